```python
import math
import jax, jax.numpy as jnp
from jax import lax
import numpy as np

D_MODEL = 1024
BATCH = 16
SEQ = 2048
DEPTH = 2
DEC_BATCH = 32
DEC_SEQ = 1
PAST_LEN = 16384
PAGE_SIZE = 128

N_MIXERS = 2
N_LAYERS_A = (DEPTH + 1) // 2
N_LAYERS_B = DEPTH // 2
A_HEADS = 8
A_DH = D_MODEL // (2 * A_HEADS)
B_HEADS = 16
B_DH = D_MODEL // B_HEADS
N_BUCKETS = 32
MAX_DISTANCE = 128
N_EXPERTS = 32
TOP_K = 4
D_EXPERT = D_MODEL
SWIGLU_LIMIT = 7.0
SWIGLU_ALPHA = 1.702
DN_ALPHA = (2 * DEPTH) ** 0.25
DN_BETA = (8 * DEPTH) ** -0.25
LN_EPS = 1e-5
Q_BLOCK = 128

kernel_name = "hybrid_diffattn_stickbreak_moe_decoder_step"


def _layer_norm(x, g, b):
    xf = x.astype(jnp.float32)
    mu = jnp.mean(xf, axis=-1, keepdims=True)
    var = jnp.mean(jnp.square(xf - mu), axis=-1, keepdims=True)
    return ((xf - mu) * lax.rsqrt(var + LN_EPS)).astype(x.dtype) * g + b


def _rms_norm(x, g):
    xf = x.astype(jnp.float32)
    return (xf * lax.rsqrt(jnp.mean(xf * xf, axis=-1, keepdims=True) + LN_EPS)).astype(x.dtype) * g


def _adaln(c, w, b):
    m = jax.nn.silu(c) @ w + b
    return jnp.split(m[:, None, :], 6, axis=-1)


def _rel_bucket(dist):
    n = jnp.maximum(dist, 0)
    max_exact = N_BUCKETS // 2
    nf = jnp.maximum(n, 1).astype(jnp.float32)
    large = max_exact + (jnp.log(nf / max_exact) / math.log(MAX_DISTANCE / max_exact)
                         * (N_BUCKETS - max_exact)).astype(jnp.int32)
    large = jnp.minimum(large, N_BUCKETS - 1)
    return jnp.where(n < max_exact, n, large)


def _rel_bias(rel_table, dist):
    return jnp.transpose(rel_table[_rel_bucket(dist)], (2, 3, 0, 1)).astype(jnp.float32)


def _gather_pages(cache, layer, page_table):
    g = cache[layer, page_table]
    return g.reshape(g.shape[0], g.shape[1] * g.shape[2], *g.shape[3:])


def _weighted_values(a, v_segs):
    out, start = 0.0, 0
    for v in v_segs:
        n = v.shape[1]
        out = out + jnp.einsum("bhqk,bkhe->bqhe", a[..., start:start + n].astype(v.dtype), v)
        start += n
    return out


def _sweep_queries(attend, q, q_pos0):
    b, t = q.shape[0], q.shape[1]
    if t % Q_BLOCK != 0:
        return attend(q, q_pos0 + jnp.arange(t, dtype=jnp.int32))
    nb = t // Q_BLOCK
    q_blocks = jnp.moveaxis(q.reshape(b, nb, Q_BLOCK, *q.shape[2:]), 1, 0)
    pos_blocks = q_pos0 + jnp.arange(t, dtype=jnp.int32).reshape(nb, Q_BLOCK)
    o = lax.map(lambda blk: attend(blk[0], blk[1]), (q_blocks, pos_blocks))
    o = jnp.moveaxis(o, 0, 1)
    return o.reshape(b, t, *o.shape[3:])


def _diff_lambda(lam_vecs, lam_init):
    lv = lam_vecs.astype(jnp.float32)
    return jnp.exp(jnp.sum(lv[0] * lv[1])) - jnp.exp(jnp.sum(lv[2] * lv[3])) + lam_init


def _diff_core(q, k_segs, v_segs, bias, mask, lam):
    s = jnp.concatenate([jnp.einsum("bqmhd,bkmhd->bmhqk", q, k, preferred_element_type=jnp.float32)
                         for k in k_segs], axis=-1)
    s = jnp.where(mask, s * (A_DH ** -0.5) + bias, -jnp.inf)
    p = jax.nn.softmax(s, axis=-1)
    a = p[:, 0] - lam * p[:, 1]
    return _weighted_values(a, v_segs)


def _diff_mixer(h, k_past, v_past, w_qkv, w_o, subln_g, rel_table, lam, lam_init):
    b, t, _ = h.shape
    q, k, v = jnp.split(h @ w_qkv, 3, axis=-1)
    q = q.reshape(b, t, 2, A_HEADS, A_DH)
    k = k.reshape(b, t, 2, A_HEADS, A_DH)
    v = v.reshape(b, t, A_HEADS, 2 * A_DH)
    k_segs = (k,) if k_past is None else (k_past, k)
    v_segs = (v,) if v_past is None else (v_past, v)
    n_keys = sum(s_.shape[1] for s_ in k_segs)
    k_pos = jnp.arange(n_keys, dtype=jnp.int32)

    def attend(q_blk, q_pos):
        dist = q_pos[:, None] - k_pos[None, :]
        return _diff_core(q_blk, k_segs, v_segs, _rel_bias(rel_table, dist), dist >= 0, lam)

    o = _sweep_queries(attend, q, n_keys - t)
    o = _rms_norm(o, subln_g) * (1.0 - lam_init)
    return o.reshape(b, t, D_MODEL) @ w_o, k, v


def _sb_core(q, k_segs, v_segs, mask):
    z = jnp.concatenate([jnp.einsum("bqhd,bkhd->bhqk", q, k, preferred_element_type=jnp.float32)
                         for k in k_segs], axis=-1) * (B_DH ** -0.5)
    log_stay = jnp.where(mask, -jax.nn.softplus(z), 0.0)
    later = lax.cumsum(log_stay, axis=z.ndim - 1, reverse=True) - log_stay
    w = jnp.where(mask, jnp.exp(jax.nn.log_sigmoid(z) + later), 0.0)
    return _weighted_values(w, v_segs)


def _sb_mixer(h, k_past, v_past, w_qkv, w_o):
    b, t, _ = h.shape
    q, k, v = jnp.split(h @ w_qkv, 3, axis=-1)
    q = q.reshape(b, t, B_HEADS, B_DH)
    k = k.reshape(b, t, B_HEADS, B_DH)
    v = v.reshape(b, t, B_HEADS, B_DH)
    k_segs = (k,) if k_past is None else (k_past, k)
    v_segs = (v,) if v_past is None else (v_past, v)
    n_keys = sum(s_.shape[1] for s_ in k_segs)
    k_pos = jnp.arange(n_keys, dtype=jnp.int32)

    def attend(q_blk, q_pos):
        return _sb_core(q_blk, k_segs, v_segs, k_pos[None, :] < q_pos[:, None])

    o = _sweep_queries(attend, q, n_keys - t)
    return o.reshape(b, t, D_MODEL) @ w_o, k, v


def _moe(h, w_router, b_router, w_gate_up, b_gate_up, w_down, b_down):
    b, t, d = h.shape
    tok = h.reshape(b * t, d)
    logits = jnp.dot(tok, w_router, preferred_element_type=jnp.float32) + b_router
    top_v, top_i = lax.top_k(logits, TOP_K)
    probs = jax.nn.softmax(top_v, axis=-1)
    combine = jnp.sum(jax.nn.one_hot(top_i, N_EXPERTS, dtype=jnp.float32) * probs[..., None], axis=1)

    def expert(acc, p):
        wgu, bgu, wd, bd, ce = p
        gu = tok @ wgu + bgu
        gate = jnp.minimum(gu[:, :D_EXPERT], SWIGLU_LIMIT)
        up = jnp.clip(gu[:, D_EXPERT:], -SWIGLU_LIMIT, SWIGLU_LIMIT)
        act = (up + 1.0) * (gate * jax.nn.sigmoid(SWIGLU_ALPHA * gate))
        y = act @ wd + bd
        return acc + ce[:, None].astype(y.dtype) * y, None

    out, _ = lax.scan(expert, jnp.zeros_like(tok), (w_gate_up, b_gate_up, w_down, b_down, combine.T))
    return out.reshape(b, t, d)


def setup_inputs(seed: int = 0) -> dict:
    key = jax.random.key(seed)
    ks = jax.random.split(key, 32)
    f32 = jnp.float32
    D = D_MODEL
    n_pages = PAST_LEN // PAGE_SIZE
    n_used = DEC_BATCH * n_pages
    n_pool = n_used + -(-n_used // 4)

    def nrm(k, shape, scale):
        return jax.random.normal(k, shape, f32) * scale

    page_table = jax.random.permutation(ks[8], n_pool)[:n_used].reshape(DEC_BATCH, n_pages).astype(jnp.int32)
    w_qkv_a = jnp.concatenate([nrm(ks[10], (N_LAYERS_A, D, 2 * D), D ** -0.5),
                               nrm(ks[11], (N_LAYERS_A, D, D), D ** -0.5 * DN_BETA)], axis=-1)
    w_qkv_b = jnp.concatenate([nrm(ks[15], (N_LAYERS_B, D, 2 * D), D ** -0.5),
                               nrm(ks[16], (N_LAYERS_B, D, D), D ** -0.5 * DN_BETA)], axis=-1)
    return {
        "x_prompt": nrm(ks[0], (BATCH, SEQ, D), 1.0),
        "x_sample": nrm(ks[1], (DEC_BATCH, DEC_SEQ, D), 1.0),
        "c_prompt": nrm(ks[2], (BATCH, D), 1.0),
        "c_sample": nrm(ks[3], (DEC_BATCH, D), 1.0),
        "cache_a_k": nrm(ks[4], (N_LAYERS_A, n_pool, PAGE_SIZE, 2, A_HEADS, A_DH), 1.0),
        "cache_a_v": nrm(ks[5], (N_LAYERS_A, n_pool, PAGE_SIZE, A_HEADS, 2 * A_DH), 1.0),
        "cache_b_k": nrm(ks[6], (N_LAYERS_B, n_pool, PAGE_SIZE, B_HEADS, B_DH), 1.0),
        "cache_b_v": nrm(ks[7], (N_LAYERS_B, n_pool, PAGE_SIZE, B_HEADS, B_DH), 1.0),
        "page_table": page_table,
        "rel_table": nrm(ks[9], (N_BUCKETS, 2, A_HEADS), 0.5),
        "w_qkv_a": w_qkv_a,
        "w_o_a": nrm(ks[12], (N_LAYERS_A, D, D), D ** -0.5 * DN_BETA),
        "lambda_a": nrm(ks[13], (N_LAYERS_A, 4, A_DH), 0.1),
        "subln_a": 1.0 + nrm(ks[14], (N_LAYERS_A, 2 * A_DH), 0.02),
        "w_qkv_b": w_qkv_b,
        "w_o_b": nrm(ks[17], (N_LAYERS_B, D, D), D ** -0.5 * DN_BETA),
        "w_ada": nrm(ks[18], (DEPTH, D, 6 * D), 0.5 * D ** -0.5),
        "b_ada": nrm(ks[19], (DEPTH, 6 * D), 0.02),
        "ln_g": 1.0 + nrm(ks[20], (DEPTH, 2, D), 0.02),
        "ln_b": nrm(ks[21], (DEPTH, 2, D), 0.02),
        "w_router": nrm(ks[22], (DEPTH, D, N_EXPERTS), D ** -0.5),
        "b_router": nrm(ks[23], (DEPTH, N_EXPERTS), 0.01),
        "w_gate_up": nrm(ks[24], (DEPTH, N_EXPERTS, D, 2 * D_EXPERT), D ** -0.5),
        "b_gate_up": nrm(ks[25], (DEPTH, N_EXPERTS, 2 * D_EXPERT), 0.02),
        "w_down": nrm(ks[26], (DEPTH, N_EXPERTS, D_EXPERT, D), D_EXPERT ** -0.5 * DN_BETA),
        "b_down": nrm(ks[27], (DEPTH, N_EXPERTS, D), 0.02),
    }


def reference(x_prompt, x_sample, c_prompt, c_sample, cache_a_k, cache_a_v, cache_b_k, cache_b_v,
              page_table, rel_table, w_qkv_a, w_o_a, lambda_a, subln_a, w_qkv_b, w_o_b,
              w_ada, b_ada, ln_g, ln_b, w_router, b_router, w_gate_up, b_gate_up, w_down, b_down):
    xp, xs = x_prompt, x_sample
    a_kp, a_vp, a_ks, a_vs = [], [], [], []
    b_kp, b_vp, b_ks, b_vs = [], [], [], []
    for i in range(DEPTH):
        j = i // N_MIXERS
        mod_p = _adaln(c_prompt, w_ada[i], b_ada[i])
        mod_s = _adaln(c_sample, w_ada[i], b_ada[i])
        hp = xp * (1.0 + mod_p[1]) + mod_p[0]
        hs = xs * (1.0 + mod_s[1]) + mod_s[0]
        if i % N_MIXERS == 0:
            lam_init = 0.8 - 0.6 * math.exp(-0.3 * i)
            lam = _diff_lambda(lambda_a[j], lam_init)
            mix_p, kp, vp = _diff_mixer(hp, None, None, w_qkv_a[j], w_o_a[j], subln_a[j], rel_table, lam, lam_init)
            mix_s, ks_, vs_ = _diff_mixer(hs, _gather_pages(cache_a_k, j, page_table),
                                          _gather_pages(cache_a_v, j, page_table),
                                          w_qkv_a[j], w_o_a[j], subln_a[j], rel_table, lam, lam_init)
            a_kp.append(kp); a_vp.append(vp); a_ks.append(ks_); a_vs.append(vs_)
        else:
            mix_p, kp, vp = _sb_mixer(hp, None, None, w_qkv_b[j], w_o_b[j])
            mix_s, ks_, vs_ = _sb_mixer(hs, _gather_pages(cache_b_k, j, page_table),
                                        _gather_pages(cache_b_v, j, page_table), w_qkv_b[j], w_o_b[j])
            b_kp.append(kp); b_vp.append(vp); b_ks.append(ks_); b_vs.append(vs_)
        xp = _layer_norm(DN_ALPHA * xp + mod_p[2] * mix_p, ln_g[i, 0], ln_b[i, 0])
        xs = _layer_norm(DN_ALPHA * xs + mod_s[2] * mix_s, ln_g[i, 0], ln_b[i, 0])
        hp = xp * (1.0 + mod_p[4]) + mod_p[3]
        hs = xs * (1.0 + mod_s[4]) + mod_s[3]
        ffn_p = _moe(hp, w_router[i], b_router[i], w_gate_up[i], b_gate_up[i], w_down[i], b_down[i])
        ffn_s = _moe(hs, w_router[i], b_router[i], w_gate_up[i], b_gate_up[i], w_down[i], b_down[i])
        xp = _layer_norm(DN_ALPHA * xp + mod_p[5] * ffn_p, ln_g[i, 1], ln_b[i, 1])
        xs = _layer_norm(DN_ALPHA * xs + mod_s[5] * ffn_s, ln_g[i, 1], ln_b[i, 1])
    return (xp, xs,
            jnp.stack(a_kp), jnp.stack(a_vp), jnp.stack(b_kp), jnp.stack(b_vp),
            jnp.stack(a_ks), jnp.stack(a_vs), jnp.stack(b_ks), jnp.stack(b_vs))
```

```python
import functools
import math

import jax
import jax.numpy as jnp
import numpy as np
from jax import lax
from jax.experimental import pallas as pl
from jax.experimental.pallas import tpu as pltpu

N_BUCKETS = 32
MAX_DISTANCE = 128
TOP_K = 4
SWIGLU_LIMIT = 7.0
SWIGLU_ALPHA = 1.702
LN_EPS = 1e-5
NEG_BIG = -1e30

LANES = 128
SUBLANES = 8
VMEM_LIMIT_BYTES = 56 * 1024 * 1024

F32 = jnp.float32
BF16 = jnp.bfloat16


def _bucket_lower_bounds():
    max_exact = N_BUCKETS // 2
    lo = list(range(max_exact))
    n = max_exact
    for b in range(max_exact, N_BUCKETS):
        while True:
            large = max_exact + int(math.log(n / max_exact) / math.log(MAX_DISTANCE / max_exact)
                                    * (N_BUCKETS - max_exact))
            if min(large, N_BUCKETS - 1) >= b:
                break
            n += 1
        lo.append(n)
    return lo


BUCKET_LO = _bucket_lower_bounds()


def _cparams(sem, vmem=VMEM_LIMIT_BYTES):
    return pltpu.CompilerParams(dimension_semantics=sem, vmem_limit_bytes=vmem)


def _split3(a):
    a1 = a.astype(BF16)
    r = a - a1.astype(F32)
    a2 = r.astype(BF16)
    a3 = (r - a2.astype(F32)).astype(BF16)
    return a1, a2, a3


def _dot(a, b):
    return jnp.dot(a, b, preferred_element_type=F32)


def _dot_nt(a, b):
    return lax.dot_general(a, b, (((1,), (1,)), ((), ())), preferred_element_type=F32)


def _dot_precise(a, b):
    a1, a2, a3 = _split3(a)
    b1, b2, b3 = _split3(b)
    out = _dot(a1, b1)
    out += _dot(a1, b2) + _dot(a2, b1)
    out += _dot(a1, b3) + _dot(a2, b2) + _dot(a3, b1)
    return out


def _mod_spec(per_row, tm, tiles_per_group):
    if per_row:
        return lambda d: pl.BlockSpec((tm, d), lambda i, *_: (i, 0))
    return lambda d: pl.BlockSpec((None, 1, d), lambda i, *_: (i // tiles_per_group, 0, 0))


def _ada_kernel(c_ref, w_ref, b_ref, o_ref):
    c = c_ref[...]
    s = c * jax.nn.sigmoid(c)
    o_ref[...] = _dot_precise(s, w_ref[...]) + b_ref[...]


def _ada(c_all, w_ada, b_ada):
    depth, d, n6 = w_ada.shape
    nb = c_all.shape[0]
    tn = 1024
    return pl.pallas_call(
        _ada_kernel,
        grid=(depth, n6 // tn),
        in_specs=[
            pl.BlockSpec((nb, d), lambda l, j: (0, 0)),
            pl.BlockSpec((None, d, tn), lambda l, j: (l, 0, j)),
            pl.BlockSpec((None, 1, tn), lambda l, j: (l, 0, j)),
        ],
        out_specs=pl.BlockSpec((None, nb, tn), lambda l, j: (l, 0, j)),
        out_shape=jax.ShapeDtypeStruct((depth, nb, n6), F32),
        compiler_params=_cparams(("arbitrary", "arbitrary")),
        name="ada",
    )(c_all, w_ada, b_ada.reshape(depth, 1, n6))


def _qkv_kernel(x_ref, sc_ref, sh_ref, w_ref, q_ref, kb_ref, vb_ref, k_ref, v_ref, *, q_scale):
    d = x_ref.shape[1]
    h = (x_ref[...] * (1.0 + sc_ref[...]) + sh_ref[...]).astype(BF16)
    q = _dot(h, w_ref[:, 0:d])
    q_ref[...] = (q * q_scale).astype(BF16)
    k = _dot(h, w_ref[:, d:2 * d])
    k_ref[...] = k
    kb_ref[...] = k.astype(BF16)
    v = _dot(h, w_ref[:, 2 * d:3 * d])
    v_ref[...] = v
    vb_ref[...] = v.astype(BF16)


def _qkv(x, sc, sh, w_bf, *, q_scale, per_row, tm, tiles_per_group):
    n, d = x.shape
    mod = _mod_spec(per_row, tm, tiles_per_group)
    row = pl.BlockSpec((tm, d), lambda i: (i, 0))
    return pl.pallas_call(
        functools.partial(_qkv_kernel, q_scale=q_scale),
        grid=(n // tm,),
        in_specs=[row, mod(d), mod(d), pl.BlockSpec((d, 3 * d), lambda i: (0, 0))],
        out_specs=[row] * 5,
        out_shape=[jax.ShapeDtypeStruct((n, d), BF16)] * 3 + [jax.ShapeDtypeStruct((n, d), F32)] * 2,
        compiler_params=_cparams(("arbitrary",)),
        name="qkv",
    )(x, sc, sh, w_bf)


def _bias_from_dist(dist, table_value):
    val = table_value(N_BUCKETS - 1)
    for b in range(N_BUCKETS - 2, -1, -1):
        val = jnp.where(dist < BUCKET_LO[b + 1], table_value(b), val)
    return val


def _diff_lambda(lam_ref, lam_init):
    lv = lam_ref[...]
    a = jnp.sum(lv[0:1, :] * lv[1:2, :], axis=1, keepdims=True)
    b = jnp.sum(lv[2:3, :] * lv[3:4, :], axis=1, keepdims=True)
    return jnp.exp(a) - jnp.exp(b) + lam_init


def _attn_a_kernel(tab_ref, lam_ref, g_ref, q0_ref, q1_ref, k0_ref, k1_ref, v_ref, o_ref, bias_scr,
                   *, lam_init, t, n_heads):
    hp = pl.program_id(0)
    b = pl.program_id(1)
    qi = pl.program_id(2)
    dh2 = v_ref.shape[1] // 2

    @pl.when((b == 0) & (qi == 0))
    def _build_bias():
        row = lax.broadcasted_iota(jnp.int32, (t, t), 0)
        col = lax.broadcasted_iota(jnp.int32, (t, t), 1)
        for delta in range(2):
            dist = row - col + delta * t
            for m in range(2):
                for hh in range(2):
                    c = m * n_heads + 2 * hp + hh
                    val = _bias_from_dist(dist, lambda bkt: tab_ref[bkt, c])
                    if delta == 0:
                        val = jnp.where(dist < 0, NEG_BIG, val)
                    bias_scr[m, hh, delta] = val
        for m in range(2):
            for hh in range(2):
                c = m * n_heads + 2 * hp + hh
                bias_scr[m, hh, 2] = jnp.full((t, t), tab_ref[N_BUCKETS - 1, c], F32)

    lam = _diff_lambda(lam_ref, lam_init)
    lane = lax.broadcasted_iota(jnp.int32, (t, LANES), 1)
    k_refs = (k0_ref, k1_ref)
    for hh in range(2):
        in_head = (lane // (LANES // 2)) == hh
        qm = [jnp.where(in_head, q_ref[...], jnp.zeros((), BF16)) for q_ref in (q0_ref, q1_ref)]

        def body(kb, carry):
            start = pl.multiple_of(kb * t, t)
            didx = jnp.minimum(qi - kb, 2)
            vt = v_ref[pl.ds(start, t), hh * dh2:(hh + 1) * dh2]
            new = []
            for m in range(2):
                mx, l, acc = carry[m]
                s = _dot_nt(qm[m], k_refs[m][pl.ds(start, t), :]) + bias_scr[m, hh, didx]
                mx_new = jnp.maximum(mx, jnp.max(s, axis=1, keepdims=True))
                alpha = jnp.exp(mx - mx_new)
                p = jnp.exp(s - mx_new)
                l = alpha * l + jnp.sum(p, axis=1, keepdims=True)
                acc = alpha * acc + _dot(p.astype(BF16), vt)
                new.append((mx_new, l, acc))
            return tuple(new)

        init = tuple((jnp.full((t, 1), NEG_BIG, F32), jnp.zeros((t, 1), F32), jnp.zeros((t, dh2), F32))
                     for _ in range(2))
        (_, l1, a1), (_, l2, a2) = lax.fori_loop(0, qi + 1, body, init)
        o = a1 / l1 - lam * (a2 / l2)
        ms = jnp.mean(o * o, axis=1, keepdims=True)
        o = (o * lax.rsqrt(ms + LN_EPS)) * g_ref[...] * (1.0 - lam_init)
        o_ref[:, hh * dh2:(hh + 1) * dh2] = o.astype(BF16)


def _attn_a(qb, kb, vb, rel_tab, lam_vecs, subln_g, *, batch, seq, n_heads, lam_init, t):
    n, d = qb.shape
    nq = seq // t
    npair = n_heads // 2
    tab2d = rel_tab.reshape(N_BUCKETS, 2 * n_heads)
    kernel = functools.partial(_attn_a_kernel, lam_init=lam_init, t=t, n_heads=n_heads)
    qspec = lambda off: pl.BlockSpec((t, LANES), lambda hp, b, qi: (b * nq + qi, off + hp))
    kspec = lambda off: pl.BlockSpec((seq, LANES), lambda hp, b, qi: (b, off + hp))
    return pl.pallas_call(
        kernel,
        grid=(npair, batch, nq),
        in_specs=[
            pl.BlockSpec(memory_space=pltpu.SMEM),
            pl.BlockSpec((4, lam_vecs.shape[1]), lambda hp, b, qi: (0, 0)),
            pl.BlockSpec((1, subln_g.shape[1]), lambda hp, b, qi: (0, 0)),
            qspec(0), qspec(npair), kspec(0), kspec(npair),
            pl.BlockSpec((seq, 2 * LANES), lambda hp, b, qi: (b, hp)),
        ],
        out_specs=pl.BlockSpec((t, 2 * LANES), lambda hp, b, qi: (b * nq + qi, hp)),
        out_shape=jax.ShapeDtypeStruct((n, d), BF16),
        scratch_shapes=[pltpu.VMEM((2, 2, 3, t, t), F32)],
        compiler_params=_cparams(("arbitrary", "arbitrary", "arbitrary")),
        name="attn_a",
    )(tab2d, lam_vecs, subln_g, qb, qb, kb, kb, vb)


def _softplus(z):
    return jnp.maximum(z, 0.0) + jnp.log1p(jnp.exp(-jnp.abs(z)))


def _suffix_sums(ls, upper):
    hi = ls.astype(BF16)
    lo = (ls - hi.astype(F32)).astype(BF16)
    return _dot(hi, upper) + _dot(lo, upper)


def _attn_b_kernel(q_ref, k_ref, v_ref, o_ref, *, t):
    qi = pl.program_id(2)
    lane = lax.broadcasted_iota(jnp.int32, (t, LANES), 1)
    row = lax.broadcasted_iota(jnp.int32, (t, t), 0)
    col = lax.broadcasted_iota(jnp.int32, (t, t), 1)
    upper = (row > col).astype(BF16)
    outs = []
    for hh in range(2):
        in_head = (lane // (LANES // 2)) == hh
        qm = jnp.where(in_head, q_ref[...], jnp.zeros((), BF16))

        def body(i, carry):
            tot, acc = carry
            kb = qi - i
            start = pl.multiple_of(kb * t, t)
            z = _dot_nt(qm, k_ref[pl.ds(start, t), :])
            valid = (col + kb * t) < (row + qi * t)
            sp = _softplus(z)
            ls = jnp.where(valid, -sp, 0.0)
            later = tot + _suffix_sums(ls, upper)
            w = jnp.where(valid, jnp.exp(z - sp + later), 0.0)
            acc = acc + _dot(w.astype(BF16), v_ref[pl.ds(start, t), :])
            tot = tot + jnp.sum(ls, axis=1, keepdims=True)
            return tot, acc

        _, acc = lax.fori_loop(0, qi + 1, body, (jnp.zeros((t, 1), F32), jnp.zeros((t, LANES), F32)))
        outs.append(acc)
    o_ref[...] = jnp.where(lane < LANES // 2, outs[0], outs[1]).astype(BF16)


def _attn_b(qb, kb, vb, *, batch, seq, n_heads, t):
    n, d = qb.shape
    nq = seq // t
    npair = n_heads // 2
    tile = pl.BlockSpec((t, LANES), lambda hp, b, qi: (b * nq + qi, hp))
    full = pl.BlockSpec((seq, LANES), lambda hp, b, qi: (b, hp))
    return pl.pallas_call(
        functools.partial(_attn_b_kernel, t=t),
        grid=(npair, batch, nq),
        in_specs=[tile, full, full],
        out_specs=tile,
        out_shape=jax.ShapeDtypeStruct((n, d), BF16),
        compiler_params=_cparams(("arbitrary", "arbitrary", "arbitrary")),
        name="attn_b",
    )(qb, kb, vb)


def _layer_norm(z, g, b):
    mu = jnp.mean(z, axis=1, keepdims=True)
    zc = z - mu
    var = jnp.mean(zc * zc, axis=1, keepdims=True)
    return (zc * lax.rsqrt(var + LN_EPS)) * g + b


def _proj_ln_kernel(o_ref, w_ref, x_ref, gate_ref, g_ref, b_ref, out_ref, *, dn_alpha):
    y = _dot(o_ref[...], w_ref[...])
    z = dn_alpha * x_ref[...] + gate_ref[...] * y
    out_ref[...] = _layer_norm(z, g_ref[...], b_ref[...])


def _proj_ln(o, w_bf, x, gate, g, b, *, dn_alpha, per_row, tm, tiles_per_group):
    n, d = x.shape
    mod = _mod_spec(per_row, tm, tiles_per_group)
    row = pl.BlockSpec((tm, d), lambda i: (i, 0))
    vec = pl.BlockSpec((1, d), lambda i: (0, 0))
    return pl.pallas_call(
        functools.partial(_proj_ln_kernel, dn_alpha=dn_alpha),
        grid=(n // tm,),
        in_specs=[row, pl.BlockSpec((d, d), lambda i: (0, 0)), row, mod(d), vec, vec],
        out_specs=row,
        out_shape=jax.ShapeDtypeStruct((n, d), F32),
        compiler_params=_cparams(("arbitrary",)),
        name="proj_ln",
    )(o, w_bf, x, gate, g, b)


def _router_kernel(x_ref, sc_ref, sh_ref, w_ref, b_ref, idx_ref, p_ref, rank_ref, cnt_ref, carry_scr,
                   *, n_experts):
    i = pl.program_id(0)
    tm = x_ref.shape[0]

    @pl.when(i == 0)
    def _():
        carry_scr[...] = jnp.zeros_like(carry_scr)

    h = x_ref[...] * (1.0 + sc_ref[...]) + sh_ref[...]
    logits = _dot_precise(h, w_ref[...]) + b_ref[...]
    lane = lax.broadcasted_iota(jnp.int32, (tm, LANES), 1)
    lane_f = lane.astype(F32)
    l = jnp.where(lane < n_experts, logits, -jnp.inf)
    idx_out = jnp.zeros((tm, LANES), jnp.int32)
    top_v = []
    top_i = []
    for k in range(TOP_K):
        mk = jnp.max(l, axis=1, keepdims=True)
        ik = jnp.min(jnp.where(l == mk, lane_f, float(LANES)), axis=1, keepdims=True).astype(jnp.int32)
        top_v.append(mk)
        top_i.append(ik)
        idx_out = jnp.where(lane == k, ik, idx_out)
        l = jnp.where(lane == ik, -jnp.inf, l)
    es = [jnp.exp(v - top_v[0]) for v in top_v]
    den = es[0] + es[1] + es[2] + es[3]
    p_out = jnp.zeros((tm, LANES), F32)
    for k in range(TOP_K):
        p_out = jnp.where(lane == k, es[k] / den, p_out)

    onehot = jnp.zeros((tm, LANES), F32)
    for k in range(TOP_K):
        onehot = onehot + (lane == top_i[k]).astype(F32)
    r = lax.broadcasted_iota(jnp.int32, (tm, tm), 0)
    c = lax.broadcasted_iota(jnp.int32, (tm, tm), 1)
    lower = (c < r).astype(BF16)
    before = _dot(lower, onehot.astype(BF16)) + carry_scr[0:1, :]
    rank_out = jnp.zeros((tm, LANES), jnp.int32)
    for k in range(TOP_K):
        rk = jnp.sum(jnp.where(lane == top_i[k], before, 0.0), axis=1, keepdims=True)
        rank_out = jnp.where(lane == k, rk.astype(jnp.int32), rank_out)
    total = carry_scr[0:1, :] + jnp.sum(onehot, axis=0, keepdims=True)
    carry_scr[...] = jnp.broadcast_to(total, carry_scr.shape)
    idx_ref[...] = idx_out
    p_ref[...] = p_out
    rank_ref[...] = rank_out
    cnt_ref[...] = jnp.broadcast_to(total, cnt_ref.shape)


def _router(x, sc, sh, w_pad, b_pad, *, n_experts, per_row, tm, tiles_per_group):
    n, d = x.shape
    mod = _mod_spec(per_row, tm, tiles_per_group)
    row = pl.BlockSpec((tm, d), lambda i: (i, 0))
    lrow = pl.BlockSpec((tm, LANES), lambda i: (i, 0))
    return pl.pallas_call(
        functools.partial(_router_kernel, n_experts=n_experts),
        grid=(n // tm,),
        in_specs=[row, mod(d), mod(d), pl.BlockSpec((d, LANES), lambda i: (0, 0)),
                  pl.BlockSpec((1, LANES), lambda i: (0, 0))],
        out_specs=[lrow, lrow, lrow, pl.BlockSpec((SUBLANES, LANES), lambda i: (0, 0))],
        out_shape=[jax.ShapeDtypeStruct((n, LANES), jnp.int32), jax.ShapeDtypeStruct((n, LANES), F32),
                   jax.ShapeDtypeStruct((n, LANES), jnp.int32),
                   jax.ShapeDtypeStruct((SUBLANES, LANES), F32)],
        scratch_shapes=[pltpu.VMEM((SUBLANES, LANES), F32)],
        compiler_params=_cparams(("arbitrary",)),
        name="router",
    )(x, sc, sh, w_pad, b_pad)


def _dispatch_kernel(dest_ref, x_ref, sc_ref, sh_ref, xs_in_ref, xs_ref, hbuf, sem):
    del xs_in_ref
    tm = x_ref.shape[0]
    hbuf[...] = x_ref[...] * (1.0 + sc_ref[...]) + sh_ref[...]

    def row_copy(r, k):
        return pltpu.make_async_copy(hbuf.at[pl.ds(r, 1)], xs_ref.at[pl.ds(dest_ref[k, r], 1)], sem)

    def start(r, _):
        for k in range(TOP_K):
            row_copy(r, k).start()
        return 0

    lax.fori_loop(0, tm, start, 0)

    def wait(r, _):
        for k in range(TOP_K):
            row_copy(r, k).wait()
        return 0

    lax.fori_loop(0, tm, wait, 0)


def _dispatch(dest_t, x, sc, sh, n_rows, *, per_row, tm, tiles_per_group):
    n, d = x.shape
    mod = _mod_spec(per_row, tm, tiles_per_group)
    xs0 = jnp.zeros((n_rows, d), F32)
    return pl.pallas_call(
        _dispatch_kernel,
        grid=(n // tm,),
        in_specs=[
            pl.BlockSpec((TOP_K, tm), lambda i: (0, i), memory_space=pltpu.SMEM),
            pl.BlockSpec((tm, d), lambda i: (i, 0)), mod(d), mod(d),
            pl.BlockSpec(memory_space=pl.ANY),
        ],
        out_specs=pl.BlockSpec(memory_space=pl.ANY),
        out_shape=jax.ShapeDtypeStruct((n_rows, d), F32),
        scratch_shapes=[pltpu.VMEM((tm, d), F32), pltpu.SemaphoreType.DMA],
        input_output_aliases={4: 0},
        compiler_params=_cparams(("arbitrary",)),
        name="dispatch",
    )(dest_t, x, sc, sh, xs0)


def _experts_kernel(te_ref, nt_ref, xs_ref, wgu_ref, bgu_ref, wd_ref, bd_ref, ys_ref):
    del te_ref
    t = pl.program_id(0)
    de = wd_ref.shape[0]

    @pl.when(t < nt_ref[0])
    def _():
        x = xs_ref[...].astype(BF16)
        gu = _dot(x, wgu_ref[...]) + bgu_ref[...]
        gate = jnp.minimum(gu[:, :de], SWIGLU_LIMIT)
        up = jnp.clip(gu[:, de:], -SWIGLU_LIMIT, SWIGLU_LIMIT)
        act = (up + 1.0) * (gate * jax.nn.sigmoid(SWIGLU_ALPHA * gate))
        ys_ref[...] = _dot(act.astype(BF16), wd_ref[...]) + bd_ref[...]

    @pl.when(t >= nt_ref[0])
    def _():
        ys_ref[...] = jnp.zeros_like(ys_ref)


def _experts(tile_expert, n_tiles, xs, wgu_bf, bgu, wd_bf, bd, *, tmg):
    n_rows, d = xs.shape
    n_exp, _, de2 = wgu_bf.shape
    de = wd_bf.shape[1]
    grid_spec = pltpu.PrefetchScalarGridSpec(
        num_scalar_prefetch=2,
        grid=(n_rows // tmg,),
        in_specs=[
            pl.BlockSpec((tmg, d), lambda t, te, nt: (t, 0)),
            pl.BlockSpec((None, d, de2), lambda t, te, nt: (te[t], 0, 0)),
            pl.BlockSpec((None, 1, de2), lambda t, te, nt: (te[t], 0, 0)),
            pl.BlockSpec((None, de, d), lambda t, te, nt: (te[t], 0, 0)),
            pl.BlockSpec((None, 1, d), lambda t, te, nt: (te[t], 0, 0)),
        ],
        out_specs=pl.BlockSpec((tmg, d), lambda t, te, nt: (t, 0)),
    )
    return pl.pallas_call(
        _experts_kernel,
        grid_spec=grid_spec,
        out_shape=jax.ShapeDtypeStruct((n_rows, d), F32),
        compiler_params=_cparams(("arbitrary",)),
        name="experts",
    )(tile_expert, n_tiles, xs, wgu_bf, bgu.reshape(n_exp, 1, de2), wd_bf, bd.reshape(n_exp, 1, d))


def _combine_kernel(dest_ref, dest_next_ref, p_ref, x_ref, gate_ref, g_ref, b_ref, ys_ref, out_ref,
                    buf, sems, *, dn_alpha):
    i = pl.program_id(0)
    n = pl.num_programs(0)
    tm = x_ref.shape[0]
    slot = i % 2

    def row_copy(dref, s, r, k):
        return pltpu.make_async_copy(ys_ref.at[pl.ds(dref[k, r], 1)], buf.at[s, k, pl.ds(r, 1)], sems.at[s])

    def start_tile(dref, s):
        def body(r, _):
            for k in range(TOP_K):
                row_copy(dref, s, r, k).start()
            return 0
        lax.fori_loop(0, tm, body, 0)

    @pl.when(i == 0)
    def _():
        start_tile(dest_ref, 0)

    @pl.when(i + 1 < n)
    def _():
        start_tile(dest_next_ref, 1 - slot)

    def wait_body(r, _):
        for k in range(TOP_K):
            row_copy(dest_ref, slot, r, k).wait()
        return 0

    lax.fori_loop(0, tm, wait_body, 0)

    p = p_ref[...]
    y = jnp.zeros(x_ref.shape, F32)
    for k in range(TOP_K):
        y = y + p[:, k:k + 1] * buf[slot, k]
    z = dn_alpha * x_ref[...] + gate_ref[...] * y
    out_ref[...] = _layer_norm(z, g_ref[...], b_ref[...])


def _combine(dest_t, p, x, gate, g, b, ys, *, dn_alpha, per_row, tm, tiles_per_group):
    n, d = x.shape
    nt = n // tm
    mod = _mod_spec(per_row, tm, tiles_per_group)
    row = pl.BlockSpec((tm, d), lambda i: (i, 0))
    vec = pl.BlockSpec((1, d), lambda i: (0, 0))
    return pl.pallas_call(
        functools.partial(_combine_kernel, dn_alpha=dn_alpha),
        grid=(nt,),
        in_specs=[
            pl.BlockSpec((TOP_K, tm), lambda i: (0, i), memory_space=pltpu.SMEM),
            pl.BlockSpec((TOP_K, tm), lambda i: (0, jnp.minimum(i + 1, nt - 1)), memory_space=pltpu.SMEM),
            pl.BlockSpec((tm, LANES), lambda i: (i, 0)),
            row, mod(d), vec, vec,
            pl.BlockSpec(memory_space=pl.ANY),
        ],
        out_specs=row,
        out_shape=jax.ShapeDtypeStruct((n, d), F32),
        scratch_shapes=[pltpu.VMEM((2, TOP_K, tm, d), F32), pltpu.SemaphoreType.DMA((2,))],
        compiler_params=_cparams(("arbitrary",)),
        name="combine",
    )(dest_t, dest_t, p, x, gate, g, b, ys)


def _moe(x, sc, sh, gate, ln_g, ln_b, w_router, b_router, wgu_bf, bgu, wd_bf, bd,
         *, dn_alpha, per_row, tm, tiles_per_group, tmg):
    n, d = x.shape
    n_exp = w_router.shape[1]
    w_pad = jnp.pad(w_router, ((0, 0), (0, LANES - n_exp)))
    b_pad = jnp.pad(b_router, (0, LANES - n_exp)).reshape(1, LANES)
    idx, p, rank, cnt = _router(x, sc, sh, w_pad, b_pad, n_experts=n_exp, per_row=per_row, tm=tm,
                                tiles_per_group=tiles_per_group)
    counts = cnt[0, :n_exp].astype(jnp.int32)
    padded = ((counts + tmg - 1) // tmg) * tmg
    ends = jnp.cumsum(padded)
    offsets = ends - padded
    n_rows = ((n * TOP_K + n_exp * (tmg - 1)) // tmg) * tmg
    top_i = idx[:, :TOP_K]
    dest_t = (offsets[top_i] + rank[:, :TOP_K]).T
    n_tiles = (ends[-1] // tmg).astype(jnp.int32)
    tile_ids = jnp.arange(n_rows // tmg, dtype=jnp.int32)
    tile_expert = jnp.searchsorted(ends // tmg, jnp.minimum(tile_ids, n_tiles - 1), side="right")
    tile_expert = jnp.minimum(tile_expert, n_exp - 1).astype(jnp.int32)

    xs = _dispatch(dest_t, x, sc, sh, n_rows, per_row=per_row, tm=tm, tiles_per_group=tiles_per_group)
    ys = _experts(tile_expert, n_tiles.reshape(1), xs, wgu_bf, bgu, wd_bf, bd, tmg=tmg)
    tmc = min(tm, 128)
    return _combine(dest_t, p, x, gate, ln_g, ln_b, ys, dn_alpha=dn_alpha, per_row=per_row, tm=tmc,
                    tiles_per_group=tiles_per_group * (tm // tmc))


def _expand_rows(q_flat, n_rows, group):
    d = q_flat.shape[1]
    r = lax.broadcasted_iota(jnp.int32, (n_rows, d), 0)
    c = lax.broadcasted_iota(jnp.int32, (n_rows, d), 1)
    q_rows = jnp.broadcast_to(q_flat.astype(F32), (n_rows, d))
    return jnp.where(c // group == r, q_rows, 0.0).astype(q_flat.dtype)


def _dec_a_kernel(pt_ref, q_ref, kn_ref, vn_ref, tabt_ref, lam_ref, g_ref, *rest,
                  lam_init, pages_per_step, page, past_len, n_heads):
    del pt_ref
    k_refs = rest[:pages_per_step]
    v_refs = rest[pages_per_step:2 * pages_per_step]
    o_ref = rest[2 * pages_per_step]
    m_scr, l_scr, acc_scr = rest[2 * pages_per_step + 1:]
    g = pl.program_id(1)
    ng = pl.num_programs(1)
    nr = 2 * n_heads
    d = q_ref.shape[1]
    dh = d // nr

    @pl.when(g == 0)
    def _():
        m_scr[...] = jnp.full_like(m_scr, NEG_BIG)
        l_scr[...] = jnp.zeros_like(l_scr)
        acc_scr[...] = jnp.zeros_like(acc_scr)

    qexp = _expand_rows(q_ref[...], nr, dh)
    tabt = tabt_ref[...]
    t1, t2, t3 = _split3(tabt)
    bk = lax.broadcasted_iota(jnp.int32, (LANES, page), 0)
    kl = lax.broadcasted_iota(jnp.int32, (1, page), 1)
    sr = lax.broadcasted_iota(jnp.int32, (page, page * n_heads), 0)
    scol = lax.broadcasted_iota(jnp.int32, (page, page * n_heads), 1)
    spread = (scol // n_heads == sr).astype(BF16)
    kr = lax.broadcasted_iota(jnp.int32, (nr, page * n_heads), 0)
    kc = lax.broadcasted_iota(jnp.int32, (nr, page * n_heads), 1)
    keep = (kc % n_heads) == (kr % n_heads)

    mx = m_scr[...]
    l = l_scr[...]
    acc = acc_scr[...]
    for j in range(pages_per_step):
        s = _dot(qexp, k_refs[j][...].astype(BF16))
        pos = (g * pages_per_step + j) * page + kl
        dist = past_len - pos
        bucket = _bias_from_dist(dist, lambda b: jnp.int32(b))
        onehot = (jnp.broadcast_to(bucket, (LANES, page)) == bk).astype(BF16)
        s = s + (_dot(t1, onehot) + _dot(t2, onehot) + _dot(t3, onehot))
        mx_new = jnp.maximum(mx, jnp.max(s, axis=1, keepdims=True))
        alpha = jnp.exp(mx - mx_new)
        p = jnp.exp(s - mx_new)
        l = alpha * l + jnp.sum(p, axis=1, keepdims=True)
        pe = jnp.where(keep, _dot(p.astype(BF16), spread), 0.0).astype(BF16)
        acc = alpha * acc + _dot(pe, v_refs[j][...].astype(BF16))
        mx = mx_new
    m_scr[...] = mx
    l_scr[...] = l
    acc_scr[...] = acc

    @pl.when(g == ng - 1)
    def _():
        kexp = jnp.broadcast_to(kn_ref[...], (nr, d)).astype(F32)
        s_new = jnp.sum(qexp.astype(F32) * kexp, axis=1, keepdims=True) + tabt[:, 0:1]
        mx2 = jnp.maximum(mx, s_new)
        alpha = jnp.exp(mx - mx2)
        p_new = jnp.exp(s_new - mx2)
        l2 = alpha * l + p_new
        vn = vn_ref[...].astype(F32)
        acc2 = alpha * acc + p_new * jnp.concatenate([vn, vn], axis=0)
        o = acc2 / l2
        lam = _diff_lambda(lam_ref, lam_init)
        o = o[:n_heads] - lam * o[n_heads:]
        ms = jnp.mean(o * o, axis=1, keepdims=True)
        o = (o * lax.rsqrt(ms + LN_EPS)) * g_ref[...] * (1.0 - lam_init)
        o_ref[...] = o.astype(BF16)


def _dec_a(page_table, qb, kb_new, vb_new, rel_tab, lam_vecs, subln_g, kt_pages, v_pages,
           *, layer, lam_init, n_heads, page, pages_per_step):
    s_n, n_pages = page_table.shape
    d = qb.shape[1]
    nr = 2 * n_heads
    dv = d // n_heads
    past_len = n_pages * page
    tabt = jnp.pad(rel_tab.reshape(N_BUCKETS, nr).T, ((0, 0), (0, LANES - N_BUCKETS)))
    pps = pages_per_step
    page_spec = lambda j: pl.BlockSpec((None, None, kt_pages.shape[2], page),
                                       lambda s, g, pt: (layer, pt[s, g * pps + j], 0, 0))
    kernel = functools.partial(_dec_a_kernel, lam_init=lam_init, pages_per_step=pps, page=page,
                               past_len=past_len, n_heads=n_heads)
    grid_spec = pltpu.PrefetchScalarGridSpec(
        num_scalar_prefetch=1,
        grid=(s_n, n_pages // pps),
        in_specs=[
            pl.BlockSpec((None, 1, d), lambda s, g, pt: (s, 0, 0)),
            pl.BlockSpec((None, 1, d), lambda s, g, pt: (s, 0, 0)),
            pl.BlockSpec((None, n_heads, dv), lambda s, g, pt: (s, 0, 0)),
            pl.BlockSpec((nr, LANES), lambda s, g, pt: (0, 0)),
            pl.BlockSpec((4, lam_vecs.shape[1]), lambda s, g, pt: (0, 0)),
            pl.BlockSpec((1, dv), lambda s, g, pt: (0, 0)),
        ] + [page_spec(j) for j in range(pps)] + [page_spec(j) for j in range(pps)],
        out_specs=pl.BlockSpec((None, n_heads, dv), lambda s, g, pt: (s, 0, 0)),
        scratch_shapes=[pltpu.VMEM((nr, 1), F32), pltpu.VMEM((nr, 1), F32), pltpu.VMEM((nr, dv), F32)],
    )
    out = pl.pallas_call(
        kernel,
        grid_spec=grid_spec,
        out_shape=jax.ShapeDtypeStruct((s_n, n_heads, dv), BF16),
        compiler_params=_cparams(("arbitrary", "arbitrary")),
        name="dec_a",
    )(page_table, qb.reshape(s_n, 1, d), kb_new.reshape(s_n, 1, d), vb_new.reshape(s_n, n_heads, dv),
      tabt, lam_vecs, subln_g, *([kt_pages] * pps), *([v_pages] * pps))
    return out.reshape(s_n, d)


def _dec_b_kernel(pt_ref, q_ref, *rest, pages_per_step, page, n_heads):
    del pt_ref
    k_refs = rest[:pages_per_step]
    v_refs = rest[pages_per_step:2 * pages_per_step]
    o_ref = rest[2 * pages_per_step]
    tot_scr, acc_scr = rest[2 * pages_per_step + 1:]
    g = pl.program_id(1)
    ng = pl.num_programs(1)
    d = q_ref.shape[1]
    dh = d // n_heads

    @pl.when(g == 0)
    def _():
        tot_scr[...] = jnp.zeros_like(tot_scr)
        acc_scr[...] = jnp.zeros_like(acc_scr)

    qexp = _expand_rows(q_ref[...], n_heads, dh)
    row = lax.broadcasted_iota(jnp.int32, (page, page), 0)
    col = lax.broadcasted_iota(jnp.int32, (page, page), 1)
    upper = (row > col).astype(BF16)
    tot = tot_scr[...]
    acc = acc_scr[...]
    for j in range(pages_per_step):
        z = _dot(qexp, k_refs[j][...].astype(BF16))
        sp = _softplus(z)
        ls = -sp
        later = tot + _suffix_sums(ls, upper)
        w = jnp.exp(z - sp + later)
        acc = acc + _dot_nt(w.astype(BF16), v_refs[j][...].astype(BF16))
        tot = tot + jnp.sum(ls, axis=1, keepdims=True)
    tot_scr[...] = tot
    acc_scr[...] = acc

    @pl.when(g == ng - 1)
    def _():
        r = lax.broadcasted_iota(jnp.int32, (n_heads, d), 0)
        c = lax.broadcasted_iota(jnp.int32, (n_heads, d), 1)
        o = jnp.sum(jnp.where(c // dh == r, acc, 0.0), axis=0, keepdims=True)
        o_ref[...] = o.astype(BF16)


def _dec_b(page_table, qb, kt_pages, vt_pages, *, layer, n_heads, page, pages_per_step):
    s_n, n_pages = page_table.shape
    d = qb.shape[1]
    pps = pages_per_step
    ng = n_pages // pps
    page_spec = lambda j: pl.BlockSpec((None, None, kt_pages.shape[2], page),
                                       lambda s, g, pt: (layer, pt[s, (ng - 1 - g) * pps + (pps - 1 - j)], 0, 0))
    grid_spec = pltpu.PrefetchScalarGridSpec(
        num_scalar_prefetch=1,
        grid=(s_n, ng),
        in_specs=[pl.BlockSpec((None, 1, d), lambda s, g, pt: (s, 0, 0))]
        + [page_spec(j) for j in range(pps)] + [page_spec(j) for j in range(pps)],
        out_specs=pl.BlockSpec((None, 1, d), lambda s, g, pt: (s, 0, 0)),
        scratch_shapes=[pltpu.VMEM((n_heads, 1), F32), pltpu.VMEM((n_heads, d), F32)],
    )
    out = pl.pallas_call(
        functools.partial(_dec_b_kernel, pages_per_step=pps, page=page, n_heads=n_heads),
        grid_spec=grid_spec,
        out_shape=jax.ShapeDtypeStruct((s_n, 1, d), BF16),
        compiler_params=_cparams(("arbitrary", "arbitrary")),
        name="dec_b",
    )(page_table, qb.reshape(s_n, 1, d), *([kt_pages] * pps), *([vt_pages] * pps))
    return out.reshape(s_n, d)


def kernel(x_prompt, x_sample, c_prompt, c_sample, cache_a_k, cache_a_v, cache_b_k, cache_b_v, page_table, rel_table, w_qkv_a, w_o_a, lambda_a, subln_a, w_qkv_b, w_o_b, w_ada, b_ada, ln_g, ln_b, w_router, b_router, w_gate_up, b_gate_up, w_down, b_down):
    batch, seq, d = x_prompt.shape
    s_n = x_sample.shape[0]
    depth = w_ada.shape[0]
    a_heads = cache_a_k.shape[4]
    a_dh = cache_a_k.shape[5]
    b_heads = cache_b_k.shape[3]
    b_dh = cache_b_k.shape[4]
    page = cache_a_k.shape[2]
    n_pool = cache_a_k.shape[1]
    dn_alpha = (2 * depth) ** 0.25
    n_p = batch * seq

    tm = min(512, seq)
    tiles_per_batch = seq // tm
    t_attn = min(256, seq)
    pps = min(8, page_table.shape[1])

    mods = _ada(jnp.concatenate([c_prompt, c_sample], axis=0), w_ada, b_ada)
    mods = mods.reshape(depth, batch + s_n, 6, d)

    xp = x_prompt.reshape(n_p, d)
    xs = x_sample.reshape(s_n, d)
    outs_p = {"a_k": [], "a_v": [], "b_k": [], "b_v": []}
    outs_s = {"a_k": [], "a_v": [], "b_k": [], "b_v": []}

    kt_a = jnp.transpose(cache_a_k, (0, 1, 3, 4, 5, 2)).reshape(-1, n_pool, 2 * a_heads * a_dh, page)
    v_a = cache_a_v.reshape(-1, n_pool, page * a_heads, 2 * a_dh)
    kt_b = jnp.transpose(cache_b_k, (0, 1, 3, 4, 2)).reshape(-1, n_pool, b_heads * b_dh, page)
    vt_b = jnp.transpose(cache_b_v, (0, 1, 3, 4, 2)).reshape(-1, n_pool, b_heads * b_dh, page)

    for i in range(depth):
        j = i // 2
        mp = [mods[i, :batch, c].reshape(batch, 1, d) for c in range(6)]
        ms = [mods[i, batch:, c] for c in range(6)]
        kw_p = dict(per_row=False, tm=tm, tiles_per_group=tiles_per_batch)
        kw_s = dict(per_row=True, tm=s_n, tiles_per_group=1)
        if i % 2 == 0:
            lam_init = 0.8 - 0.6 * math.exp(-0.3 * i)
            w_bf = w_qkv_a[j].astype(BF16)
            q_scale = a_dh ** -0.5
            qb, kb, vb, kf, vf = _qkv(xp, mp[1], mp[0], w_bf, q_scale=q_scale, **kw_p)
            o_p = _attn_a(qb, kb, vb, rel_table, lambda_a[j], subln_a[j].reshape(1, -1), batch=batch, seq=seq,
                          n_heads=a_heads, lam_init=lam_init, t=t_attn)
            qs, ks, vs, ksf, vsf = _qkv(xs, ms[1], ms[0], w_bf, q_scale=q_scale, **kw_s)
            o_s = _dec_a(page_table, qs, ks, vs, rel_table, lambda_a[j], subln_a[j].reshape(1, -1),
                         kt_a, v_a, layer=j, lam_init=lam_init, n_heads=a_heads, page=page, pages_per_step=pps)
            w_o = w_o_a[j].astype(BF16)
            outs_p["a_k"].append(kf.reshape(batch, seq, 2, a_heads, a_dh))
            outs_p["a_v"].append(vf.reshape(batch, seq, a_heads, 2 * a_dh))
            outs_s["a_k"].append(ksf.reshape(s_n, 1, 2, a_heads, a_dh))
            outs_s["a_v"].append(vsf.reshape(s_n, 1, a_heads, 2 * a_dh))
        else:
            w_bf = w_qkv_b[j].astype(BF16)
            q_scale = b_dh ** -0.5
            qb, kb, vb, kf, vf = _qkv(xp, mp[1], mp[0], w_bf, q_scale=q_scale, **kw_p)
            o_p = _attn_b(qb, kb, vb, batch=batch, seq=seq, n_heads=b_heads, t=t_attn)
            qs, ks, vs, ksf, vsf = _qkv(xs, ms[1], ms[0], w_bf, q_scale=q_scale, **kw_s)
            o_s = _dec_b(page_table, qs, kt_b, vt_b, layer=j, n_heads=b_heads, page=page, pages_per_step=pps)
            w_o = w_o_b[j].astype(BF16)
            outs_p["b_k"].append(kf.reshape(batch, seq, b_heads, b_dh))
            outs_p["b_v"].append(vf.reshape(batch, seq, b_heads, b_dh))
            outs_s["b_k"].append(ksf.reshape(s_n, 1, b_heads, b_dh))
            outs_s["b_v"].append(vsf.reshape(s_n, 1, b_heads, b_dh))
        g0 = ln_g[i, 0].reshape(1, d)
        b0 = ln_b[i, 0].reshape(1, d)
        g1 = ln_g[i, 1].reshape(1, d)
        b1 = ln_b[i, 1].reshape(1, d)
        xp = _proj_ln(o_p, w_o, xp, mp[2], g0, b0, dn_alpha=dn_alpha, **kw_p)
        xs = _proj_ln(o_s, w_o, xs, ms[2], g0, b0, dn_alpha=dn_alpha, **kw_s)
        wgu_bf = w_gate_up[i].astype(BF16)
        wd_bf = w_down[i].astype(BF16)
        moe_args = (w_router[i], b_router[i], wgu_bf, b_gate_up[i], wd_bf, b_down[i])
        xp = _moe(xp, mp[4], mp[3], mp[5], g1, b1, *moe_args, dn_alpha=dn_alpha, tmg=256, **kw_p)
        xs = _moe(xs, ms[4], ms[3], ms[5], g1, b1, *moe_args, dn_alpha=dn_alpha, tmg=8, **kw_s)

    return (xp.reshape(batch, seq, d), xs.reshape(s_n, 1, d),
            jnp.stack(outs_p["a_k"]), jnp.stack(outs_p["a_v"]), jnp.stack(outs_p["b_k"]), jnp.stack(outs_p["b_v"]),
            jnp.stack(outs_s["a_k"]), jnp.stack(outs_s["a_v"]), jnp.stack(outs_s["b_k"]), jnp.stack(outs_s["b_v"]))
```

```python
import functools
import math

import jax
import jax.numpy as jnp
from jax import lax
from jax.experimental import pallas as pl
from jax.experimental.pallas import tpu as pltpu

N_BUCKETS = 32
MAX_DISTANCE = 128
TOP_K = 4
SWIGLU_LIMIT = 7.0
SWIGLU_ALPHA = 1.702
LN_EPS = 1e-5
NEG_BIG = -1e30
SB_DEAD = -120.0

LANES = 128
SUBLANES = 8
VMEM_LIMIT_BYTES = 56 * 1024 * 1024

F32 = jnp.float32
BF16 = jnp.bfloat16


def _bucket_lower_bounds():
    max_exact = N_BUCKETS // 2
    lo = list(range(max_exact))
    n = max_exact
    for b in range(max_exact, N_BUCKETS):
        while True:
            large = max_exact + int(math.log(n / max_exact) / math.log(MAX_DISTANCE / max_exact)
                                    * (N_BUCKETS - max_exact))
            if min(large, N_BUCKETS - 1) >= b:
                break
            n += 1
        lo.append(n)
    return lo


BUCKET_LO = _bucket_lower_bounds()


def _cparams(sem, vmem=VMEM_LIMIT_BYTES):
    return pltpu.CompilerParams(dimension_semantics=sem, vmem_limit_bytes=vmem)


def _split3(a):
    a1 = a.astype(BF16)
    r = a - a1.astype(F32)
    a2 = r.astype(BF16)
    a3 = (r - a2.astype(F32)).astype(BF16)
    return a1, a2, a3


def _dot(a, b):
    return jnp.dot(a, b, preferred_element_type=F32)


def _dot_nt(a, b):
    return lax.dot_general(a, b, (((1,), (1,)), ((), ())), preferred_element_type=F32)


def _dot_precise(a, b):
    a1, a2, a3 = _split3(a)
    b1, b2, b3 = _split3(b)
    out = _dot(a1, b1)
    out += _dot(a1, b2) + _dot(a2, b1)
    out += _dot(a1, b3) + _dot(a2, b2) + _dot(a3, b1)
    return out


def _mod_spec(per_row, tm, tiles_per_group):
    if per_row:
        return lambda d: pl.BlockSpec((tm, d), lambda i, *_: (i, 0))
    return lambda d: pl.BlockSpec((None, 1, d), lambda i, *_: (i // tiles_per_group, 0, 0))


def _ada_kernel(c_ref, w_ref, b_ref, o_ref):
    c = c_ref[...]
    s = c * jax.nn.sigmoid(c)
    o_ref[...] = _dot_precise(s, w_ref[...]) + b_ref[...]


def _ada(c_all, w_ada, b_ada):
    depth, d, n6 = w_ada.shape
    nb = c_all.shape[0]
    tn = 1024
    return pl.pallas_call(
        _ada_kernel,
        grid=(depth, n6 // tn),
        in_specs=[
            pl.BlockSpec((nb, d), lambda l, j: (0, 0)),
            pl.BlockSpec((None, d, tn), lambda l, j: (l, 0, j)),
            pl.BlockSpec((None, 1, tn), lambda l, j: (l, 0, j)),
        ],
        out_specs=pl.BlockSpec((None, nb, tn), lambda l, j: (l, 0, j)),
        out_shape=jax.ShapeDtypeStruct((depth, nb, n6), F32),
        compiler_params=_cparams(("arbitrary", "arbitrary")),
        name="ada",
    )(c_all, w_ada, b_ada.reshape(depth, 1, n6))


def _qkv_kernel(x_ref, sc_ref, sh_ref, w_ref, q_ref, kb_ref, vb_ref, k_ref, v_ref, *, q_scale, k_t, v_t):
    d = x_ref.shape[1]
    h = (x_ref[...] * (1.0 + sc_ref[...]) + sh_ref[...]).astype(BF16)
    q = _dot(h, w_ref[:, 0:d])
    q_ref[...] = (q * q_scale).astype(BF16)
    k = _dot(h, w_ref[:, d:2 * d])
    k_ref[...] = k.T if k_t else k
    kb_ref[...] = k.astype(BF16)
    v = _dot(h, w_ref[:, 2 * d:3 * d])
    v_ref[...] = v.T if v_t else v
    vb_ref[...] = v.astype(BF16)


def _qkv(x, sc, sh, w_bf, *, q_scale, per_row, tm, tiles_per_group, k_t=False, v_t=False):
    n, d = x.shape
    mod = _mod_spec(per_row, tm, tiles_per_group)
    row = pl.BlockSpec((tm, d), lambda i: (i, 0))
    tpg = tiles_per_group
    groups = n // (tm * tpg)
    t_spec = pl.BlockSpec((None, d, tm), lambda i: (i // tpg, 0, i % tpg))
    t_shape = jax.ShapeDtypeStruct((groups, d, tm * tpg), F32)
    n_shape = jax.ShapeDtypeStruct((n, d), F32)
    return pl.pallas_call(
        functools.partial(_qkv_kernel, q_scale=q_scale, k_t=k_t, v_t=v_t),
        grid=(n // tm,),
        in_specs=[row, mod(d), mod(d), pl.BlockSpec((d, 3 * d), lambda i: (0, 0))],
        out_specs=[row] * 3 + [t_spec if k_t else row, t_spec if v_t else row],
        out_shape=[jax.ShapeDtypeStruct((n, d), BF16)] * 3 + [t_shape if k_t else n_shape,
                                                             t_shape if v_t else n_shape],
        compiler_params=_cparams(("arbitrary",)),
        name="qkv",
    )(x, sc, sh, w_bf)


def _bias_from_dist(dist, table_value):
    val = table_value(N_BUCKETS - 1)
    for b in range(N_BUCKETS - 2, -1, -1):
        val = jnp.where(dist < BUCKET_LO[b + 1], table_value(b), val)
    return val


def _diff_lambda(lam_ref, lam_init):
    lv = lam_ref[...]
    a = jnp.sum(lv[0:1, :] * lv[1:2, :], axis=1, keepdims=True)
    b = jnp.sum(lv[2:3, :] * lv[3:4, :], axis=1, keepdims=True)
    return jnp.exp(a) - jnp.exp(b) + lam_init


def _attn_a_kernel(tab_ref, lam_ref, g_ref, q0_ref, q1_ref, k0_ref, k1_ref, v_ref, o_ref, bias_scr,
                   *, lam_init, t, n_heads):
    hp = pl.program_id(0)
    b = pl.program_id(1)
    qi = pl.program_id(2)
    dh2 = v_ref.shape[1] // 2
    assert t >= BUCKET_LO[N_BUCKETS - 1]

    @pl.when((b == 0) & (qi == 0))
    def _build_bias():
        row = lax.broadcasted_iota(jnp.int32, (t, t), 0)
        col = lax.broadcasted_iota(jnp.int32, (t, t), 1)
        for delta in range(2):
            dist = row - col + delta * t
            for m in range(2):
                for hh in range(2):
                    c = m * n_heads + 2 * hp + hh
                    far = tab_ref[N_BUCKETS - 1, c]
                    val = _bias_from_dist(dist, lambda bkt: tab_ref[bkt, c] - far)
                    if delta == 0:
                        val = jnp.where(dist < 0, NEG_BIG, val)
                    bias_scr[m, hh, delta] = val

    lam = _diff_lambda(lam_ref, lam_init)
    lane = lax.broadcasted_iota(jnp.int32, (t, LANES), 1)
    k_refs = (k0_ref, k1_ref)
    chains = [(m, hh) for hh in range(2) for m in range(2)]
    qm = {}
    for hh in range(2):
        in_head = (lane // (LANES // 2)) == hh
        for m, q_ref in enumerate((q0_ref, q1_ref)):
            qm[(m, hh)] = jnp.where(in_head, q_ref[...], jnp.zeros((), BF16))

    def step(kb, carry, with_bias):
        start = pl.multiple_of(kb * t, t)
        kts = [k_refs[m][pl.ds(start, t), :] for m in range(2)]
        vts = [v_ref[pl.ds(start, t), hh * dh2:(hh + 1) * dh2] for hh in range(2)]
        new = []
        for ci, (m, hh) in enumerate(chains):
            mx, l, acc = carry[ci]
            s = _dot_nt(qm[(m, hh)], kts[m])
            if with_bias:
                s = s + bias_scr[m, hh, qi - kb]
            mx_new = jnp.maximum(mx, jnp.max(s, axis=1, keepdims=True))
            alpha = jnp.exp(mx - mx_new)
            p = jnp.exp(s - mx_new)
            l = alpha * l + jnp.sum(p, axis=1, keepdims=True)
            acc = alpha * acc + _dot(p.astype(BF16), vts[hh])
            new.append((mx_new, l, acc))
        return tuple(new)

    init = tuple((jnp.full((t, 1), NEG_BIG, F32), jnp.zeros((t, 1), F32), jnp.zeros((t, dh2), F32))
                 for _ in chains)
    n_far = jnp.maximum(qi - 1, 0)
    carry = lax.fori_loop(0, n_far, lambda kb, c: step(kb, c, False), init)
    carry = lax.fori_loop(n_far, qi + 1, lambda kb, c: step(kb, c, True), carry)
    for hh in range(2):
        (_, l1, a1), (_, l2, a2) = carry[2 * hh], carry[2 * hh + 1]
        o = a1 / l1 - lam * (a2 / l2)
        ms = jnp.mean(o * o, axis=1, keepdims=True)
        o = (o * lax.rsqrt(ms + LN_EPS)) * g_ref[...] * (1.0 - lam_init)
        o_ref[:, hh * dh2:(hh + 1) * dh2] = o.astype(BF16)


def _attn_a(qb, kb, vb, rel_tab, lam_vecs, subln_g, *, batch, seq, n_heads, lam_init, t):
    n, d = qb.shape
    nq = seq // t
    npair = n_heads // 2
    tab2d = rel_tab.reshape(N_BUCKETS, 2 * n_heads)
    kernel = functools.partial(_attn_a_kernel, lam_init=lam_init, t=t, n_heads=n_heads)
    qspec = lambda off: pl.BlockSpec((t, LANES), lambda hp, b, qi: (b * nq + qi, off + hp))
    kspec = lambda off: pl.BlockSpec((seq, LANES), lambda hp, b, qi: (b, off + hp))
    return pl.pallas_call(
        kernel,
        grid=(npair, batch, nq),
        in_specs=[
            pl.BlockSpec(memory_space=pltpu.SMEM),
            pl.BlockSpec((4, lam_vecs.shape[1]), lambda hp, b, qi: (0, 0)),
            pl.BlockSpec((1, subln_g.shape[1]), lambda hp, b, qi: (0, 0)),
            qspec(0), qspec(npair), kspec(0), kspec(npair),
            pl.BlockSpec((seq, 2 * LANES), lambda hp, b, qi: (b, hp)),
        ],
        out_specs=pl.BlockSpec((t, 2 * LANES), lambda hp, b, qi: (b * nq + qi, hp)),
        out_shape=jax.ShapeDtypeStruct((n, d), BF16),
        scratch_shapes=[pltpu.VMEM((2, 2, 2, t, t), F32)],
        compiler_params=_cparams(("arbitrary", "arbitrary", "arbitrary")),
        name="attn_a",
    )(tab2d, lam_vecs, subln_g, qb, qb, kb, kb, vb)


def _softplus(z):
    return jnp.maximum(z, 0.0) + jnp.log(1.0 + jnp.exp(-jnp.abs(z)))


def _suffix_sums(ls, upper):
    hi = ls.astype(BF16)
    lo = (ls - hi.astype(F32)).astype(BF16)
    return _dot(hi, upper) + _dot(lo, upper)


def _attn_b_kernel(q_ref, k_ref, v_ref, o_ref, *, t):
    qi = pl.program_id(2)
    lane = lax.broadcasted_iota(jnp.int32, (t, LANES), 1)
    row = lax.broadcasted_iota(jnp.int32, (t, t), 0)
    col = lax.broadcasted_iota(jnp.int32, (t, t), 1)
    upper = (row > col).astype(BF16)
    below_diag = col < row
    q = q_ref[...]
    qm = [jnp.where((lane // (LANES // 2)) == hh, q, jnp.zeros((), BF16)) for hh in range(2)]

    def block(kb, state, diagonal):
        start = pl.multiple_of(kb * t, t)
        kt = k_ref[pl.ds(start, t), :]
        vt = v_ref[pl.ds(start, t), :]
        new = []
        for hh in range(2):
            tot, acc = state[hh]
            z = _dot_nt(qm[hh], kt)
            sp = _softplus(z)
            ls = jnp.where(below_diag, -sp, 0.0) if diagonal else -sp
            w = jnp.exp(z - sp + (tot + _suffix_sums(ls, upper)))
            if diagonal:
                w = jnp.where(below_diag, w, 0.0)
            acc = acc + _dot(w.astype(BF16), vt)
            tot = tot + jnp.sum(ls, axis=1, keepdims=True)
            new.append((tot, acc))
        return tuple(new)

    zero = tuple((jnp.zeros((t, 1), F32), jnp.zeros((t, LANES), F32)) for _ in range(2))
    state = block(qi, zero, True)

    def live(c):
        kb, st = c
        return (kb >= 0) & (jnp.max(jnp.maximum(st[0][0], st[1][0])) > SB_DEAD)

    def older(c):
        kb, st = c
        return kb - 1, block(kb, st, False)

    _, state = lax.while_loop(live, older, (qi - 1, state))
    o_ref[...] = jnp.where(lane < LANES // 2, state[0][1], state[1][1]).astype(BF16)


def _attn_b(qb, kb, vb, *, batch, seq, n_heads, t):
    n, d = qb.shape
    nq = seq // t
    npair = n_heads // 2
    tile = pl.BlockSpec((t, LANES), lambda hp, b, qi: (b * nq + qi, hp))
    full = pl.BlockSpec((seq, LANES), lambda hp, b, qi: (b, hp))
    return pl.pallas_call(
        functools.partial(_attn_b_kernel, t=t),
        grid=(npair, batch, nq),
        in_specs=[tile, full, full],
        out_specs=tile,
        out_shape=jax.ShapeDtypeStruct((n, d), BF16),
        compiler_params=_cparams(("arbitrary", "arbitrary", "arbitrary")),
        name="attn_b",
    )(qb, kb, vb)


def _layer_norm(z, g, b):
    mu = jnp.mean(z, axis=1, keepdims=True)
    zc = z - mu
    var = jnp.mean(zc * zc, axis=1, keepdims=True)
    return (zc * lax.rsqrt(var + LN_EPS)) * g + b


def _proj_ln_kernel(o_ref, w_ref, x_ref, gate_ref, g_ref, b_ref, out_ref, *, dn_alpha):
    y = _dot(o_ref[...], w_ref[...])
    z = dn_alpha * x_ref[...] + gate_ref[...] * y
    out_ref[...] = _layer_norm(z, g_ref[...], b_ref[...])


def _proj_ln(o, w_bf, x, gate, g, b, *, dn_alpha, per_row, tm, tiles_per_group):
    n, d = x.shape
    mod = _mod_spec(per_row, tm, tiles_per_group)
    row = pl.BlockSpec((tm, d), lambda i: (i, 0))
    vec = pl.BlockSpec((1, d), lambda i: (0, 0))
    return pl.pallas_call(
        functools.partial(_proj_ln_kernel, dn_alpha=dn_alpha),
        grid=(n // tm,),
        in_specs=[row, pl.BlockSpec((d, d), lambda i: (0, 0)), row, mod(d), vec, vec],
        out_specs=row,
        out_shape=jax.ShapeDtypeStruct((n, d), F32),
        compiler_params=_cparams(("arbitrary",)),
        name="proj_ln",
    )(o, w_bf, x, gate, g, b)


def _router_kernel(x_ref, sc_ref, sh_ref, w_ref, b_ref, idx_ref, p_ref, rank_ref, cnt_ref, carry_scr,
                   *, n_experts):
    i = pl.program_id(0)
    tm = x_ref.shape[0]

    @pl.when(i == 0)
    def _():
        carry_scr[...] = jnp.zeros_like(carry_scr)

    h = x_ref[...] * (1.0 + sc_ref[...]) + sh_ref[...]
    logits = _dot_precise(h, w_ref[...]) + b_ref[...]
    lane = lax.broadcasted_iota(jnp.int32, (tm, LANES), 1)
    lane_f = lane.astype(F32)
    l = jnp.where(lane < n_experts, logits, -jnp.inf)
    idx_out = jnp.zeros((tm, LANES), jnp.int32)
    top_v = []
    top_i = []
    for k in range(TOP_K):
        mk = jnp.max(l, axis=1, keepdims=True)
        ik = jnp.min(jnp.where(l == mk, lane_f, float(LANES)), axis=1, keepdims=True).astype(jnp.int32)
        top_v.append(mk)
        top_i.append(ik)
        idx_out = jnp.where(lane == k, ik, idx_out)
        l = jnp.where(lane == ik, -jnp.inf, l)
    es = [jnp.exp(v - top_v[0]) for v in top_v]
    den = es[0] + es[1] + es[2] + es[3]
    p_out = jnp.zeros((tm, LANES), F32)
    for k in range(TOP_K):
        p_out = jnp.where(lane == k, es[k] / den, p_out)

    onehot = jnp.zeros((tm, LANES), F32)
    for k in range(TOP_K):
        onehot = onehot + (lane == top_i[k]).astype(F32)
    r = lax.broadcasted_iota(jnp.int32, (tm, tm), 0)
    c = lax.broadcasted_iota(jnp.int32, (tm, tm), 1)
    lower = (c < r).astype(BF16)
    before = _dot(lower, onehot.astype(BF16)) + carry_scr[0:1, :]
    rank_out = jnp.zeros((tm, LANES), jnp.int32)
    for k in range(TOP_K):
        rk = jnp.sum(jnp.where(lane == top_i[k], before, 0.0), axis=1, keepdims=True)
        rank_out = jnp.where(lane == k, rk.astype(jnp.int32), rank_out)
    total = carry_scr[0:1, :] + jnp.sum(onehot, axis=0, keepdims=True)
    carry_scr[...] = jnp.broadcast_to(total, carry_scr.shape)
    idx_ref[...] = idx_out
    p_ref[...] = p_out
    rank_ref[...] = rank_out
    cnt_ref[...] = jnp.broadcast_to(total, cnt_ref.shape)


def _router(x, sc, sh, w_pad, b_pad, *, n_experts, per_row, tm, tiles_per_group):
    n, d = x.shape
    mod = _mod_spec(per_row, tm, tiles_per_group)
    row = pl.BlockSpec((tm, d), lambda i: (i, 0))
    lrow = pl.BlockSpec((tm, LANES), lambda i: (i, 0))
    return pl.pallas_call(
        functools.partial(_router_kernel, n_experts=n_experts),
        grid=(n // tm,),
        in_specs=[row, mod(d), mod(d), pl.BlockSpec((d, LANES), lambda i: (0, 0)),
                  pl.BlockSpec((1, LANES), lambda i: (0, 0))],
        out_specs=[lrow, lrow, lrow, pl.BlockSpec((SUBLANES, LANES), lambda i: (0, 0))],
        out_shape=[jax.ShapeDtypeStruct((n, LANES), jnp.int32), jax.ShapeDtypeStruct((n, LANES), F32),
                   jax.ShapeDtypeStruct((n, LANES), jnp.int32),
                   jax.ShapeDtypeStruct((SUBLANES, LANES), F32)],
        scratch_shapes=[pltpu.VMEM((SUBLANES, LANES), F32)],
        compiler_params=_cparams(("arbitrary",)),
        name="router",
    )(x, sc, sh, w_pad, b_pad)


def _slab_rows(d):
    assert d % LANES == 0 and d // LANES == SUBLANES
    return d // LANES


def _store_slabs(ref, val, tm):
    for c in range(SUBLANES):
        ref[pl.ds(c, tm, stride=SUBLANES), :] = val[:, c * LANES:(c + 1) * LANES]


def _load_slabs(ref, tm, lead=()):
    return [ref[lead + (pl.ds(c, tm, stride=SUBLANES), slice(None))] for c in range(SUBLANES)]


def _dispatch_kernel(pad_start_ref, pad_len_ref, dest_ref, x_ref, sc_ref, sh_ref, xs_ref, hbuf, zbuf, sem, zsem,
                     *, n_experts):
    i = pl.program_id(0)
    tm = x_ref.shape[0]

    def zero_copy(e, j):
        row = pl.multiple_of((pad_start_ref[e] + j) * SUBLANES, SUBLANES)
        return pltpu.make_async_copy(zbuf, xs_ref.at[pl.ds(row, SUBLANES)], zsem)

    @pl.when(i == 0)
    def _():
        zbuf[...] = jnp.zeros_like(zbuf)
        for e in range(n_experts):
            lax.fori_loop(0, pad_len_ref[e], lambda j, _: (zero_copy(e, j).start(), 0)[1], 0)
        for e in range(n_experts):
            lax.fori_loop(0, pad_len_ref[e], lambda j, _: (zero_copy(e, j).wait(), 0)[1], 0)

    _store_slabs(hbuf, x_ref[...] * (1.0 + sc_ref[...]) + sh_ref[...], tm)

    def row_copy(r, k):
        src = pl.multiple_of(r * SUBLANES, SUBLANES)
        dst = pl.multiple_of(dest_ref[k, r] * SUBLANES, SUBLANES)
        return pltpu.make_async_copy(hbuf.at[pl.ds(src, SUBLANES)], xs_ref.at[pl.ds(dst, SUBLANES)], sem)

    def start(r, _):
        for k in range(TOP_K):
            row_copy(r, k).start()
        return 0

    def wait(r, _):
        for k in range(TOP_K):
            row_copy(r, k).wait()
        return 0

    lax.fori_loop(0, tm, start, 0, unroll=8)
    lax.fori_loop(0, tm, wait, 0, unroll=8)


def _dispatch(pad_start, pad_len, dest_t, x, sc, sh, n_rows, *, per_row, tm, tiles_per_group):
    n, d = x.shape
    _slab_rows(d)
    mod = _mod_spec(per_row, tm, tiles_per_group)
    grid_spec = pltpu.PrefetchScalarGridSpec(
        num_scalar_prefetch=2,
        grid=(n // tm,),
        in_specs=[
            pl.BlockSpec((TOP_K, tm), lambda i, *_: (0, i), memory_space=pltpu.SMEM),
            pl.BlockSpec((tm, d), lambda i, *_: (i, 0)), mod(d), mod(d),
        ],
        out_specs=pl.BlockSpec(memory_space=pl.ANY),
        scratch_shapes=[pltpu.VMEM((tm * SUBLANES, LANES), F32), pltpu.VMEM((SUBLANES, LANES), F32),
                        pltpu.SemaphoreType.DMA, pltpu.SemaphoreType.DMA],
    )
    return pl.pallas_call(
        functools.partial(_dispatch_kernel, n_experts=pad_start.shape[0]),
        grid_spec=grid_spec,
        out_shape=jax.ShapeDtypeStruct((n_rows * SUBLANES, LANES), F32),
        compiler_params=_cparams(("arbitrary",)),
        name="dispatch",
    )(pad_start, pad_len, dest_t, x, sc, sh)


def _experts_kernel(te_ref, nt_ref, xs_ref, wgu_ref, bgu_ref, wd_ref, bd_ref, ys_ref, wgu_bf, wd_bf, *, tmg):
    t = pl.program_id(0)
    de = wd_ref.shape[0]

    @pl.when((t == 0) | (te_ref[t] != te_ref[jnp.maximum(t - 1, 0)]))
    def _():
        wgu_bf[...] = wgu_ref[...].astype(BF16)
        wd_bf[...] = wd_ref[...].astype(BF16)

    @pl.when(t < nt_ref[0])
    def _():
        x = jnp.concatenate([c.astype(BF16) for c in _load_slabs(xs_ref, tmg)], axis=1)
        gu = _dot(x, wgu_bf[...]) + bgu_ref[...]
        gate = jnp.minimum(gu[:, :de], SWIGLU_LIMIT)
        up = jnp.clip(gu[:, de:], -SWIGLU_LIMIT, SWIGLU_LIMIT)
        act = (up + 1.0) * (gate * jax.nn.sigmoid(SWIGLU_ALPHA * gate))
        _store_slabs(ys_ref, _dot(act.astype(BF16), wd_bf[...]) + bd_ref[...], tmg)

    @pl.when(t >= nt_ref[0])
    def _():
        ys_ref[...] = jnp.zeros_like(ys_ref)


def _experts(tile_expert, n_tiles, xs, w_gate_up, b_gate_up, w_down, b_down, *, layer, tmg):
    _, n_exp, d, de2 = w_gate_up.shape
    de = w_down.shape[2]
    n_rows = xs.shape[0] // SUBLANES
    depth = w_gate_up.shape[0]
    last = lambda t, nt: jnp.minimum(t, nt[0] - 1)
    grid_spec = pltpu.PrefetchScalarGridSpec(
        num_scalar_prefetch=2,
        grid=(n_rows // tmg,),
        in_specs=[
            pl.BlockSpec((tmg * SUBLANES, LANES), lambda t, te, nt: (last(t, nt), 0)),
            pl.BlockSpec((None, None, d, de2), lambda t, te, nt: (layer, te[t], 0, 0)),
            pl.BlockSpec((None, None, 1, de2), lambda t, te, nt: (layer, te[t], 0, 0)),
            pl.BlockSpec((None, None, de, d), lambda t, te, nt: (layer, te[t], 0, 0)),
            pl.BlockSpec((None, None, 1, d), lambda t, te, nt: (layer, te[t], 0, 0)),
        ],
        out_specs=pl.BlockSpec((tmg * SUBLANES, LANES), lambda t, te, nt: (t, 0)),
        scratch_shapes=[pltpu.VMEM((d, de2), BF16), pltpu.VMEM((de, d), BF16)],
    )
    return pl.pallas_call(
        functools.partial(_experts_kernel, tmg=tmg),
        grid_spec=grid_spec,
        out_shape=jax.ShapeDtypeStruct((n_rows * SUBLANES, LANES), F32),
        compiler_params=_cparams(("arbitrary",)),
        name="experts",
    )(tile_expert, n_tiles, xs, w_gate_up, b_gate_up.reshape(depth, n_exp, 1, de2), w_down,
      b_down.reshape(depth, n_exp, 1, d))


def _combine_kernel(dest_ref, dest_next_ref, p_ref, x_ref, gate_ref, g_ref, b_ref, ys_ref, out_ref,
                    buf, sems, *, dn_alpha):
    i = pl.program_id(0)
    n = pl.num_programs(0)
    tm = x_ref.shape[0]
    slot = i % 2

    def row_copy(dref, s, r, k):
        src = pl.multiple_of(dref[k, r] * SUBLANES, SUBLANES)
        dst = pl.multiple_of(r * SUBLANES, SUBLANES)
        return pltpu.make_async_copy(ys_ref.at[pl.ds(src, SUBLANES)], buf.at[s, k, pl.ds(dst, SUBLANES)],
                                     sems.at[s])

    def start_tile(dref, s):
        def body(r, _):
            for k in range(TOP_K):
                row_copy(dref, s, r, k).start()
            return 0
        lax.fori_loop(0, tm, body, 0, unroll=8)

    @pl.when(i == 0)
    def _():
        start_tile(dest_ref, 0)

    @pl.when(i + 1 < n)
    def _():
        start_tile(dest_next_ref, 1 - slot)

    def wait_body(r, _):
        for k in range(TOP_K):
            row_copy(dest_ref, slot, r, k).wait()
        return 0

    lax.fori_loop(0, tm, wait_body, 0, unroll=8)

    p = p_ref[...]
    pk = [p[:, k:k + 1] for k in range(TOP_K)]
    chunks = [_load_slabs(buf, tm, lead=(slot, k)) for k in range(TOP_K)]
    y = jnp.concatenate([sum(pk[k] * chunks[k][c] for k in range(TOP_K)) for c in range(SUBLANES)], axis=1)
    z = dn_alpha * x_ref[...] + gate_ref[...] * y
    out_ref[...] = _layer_norm(z, g_ref[...], b_ref[...])


def _combine(dest_t, p, x, gate, g, b, ys, *, dn_alpha, per_row, tm, tiles_per_group):
    n, d = x.shape
    nt = n // tm
    mod = _mod_spec(per_row, tm, tiles_per_group)
    row = pl.BlockSpec((tm, d), lambda i: (i, 0))
    vec = pl.BlockSpec((1, d), lambda i: (0, 0))
    return pl.pallas_call(
        functools.partial(_combine_kernel, dn_alpha=dn_alpha),
        grid=(nt,),
        in_specs=[
            pl.BlockSpec((TOP_K, tm), lambda i: (0, i), memory_space=pltpu.SMEM),
            pl.BlockSpec((TOP_K, tm), lambda i: (0, jnp.minimum(i + 1, nt - 1)), memory_space=pltpu.SMEM),
            pl.BlockSpec((tm, LANES), lambda i: (i, 0)),
            row, mod(d), vec, vec,
            pl.BlockSpec(memory_space=pl.ANY),
        ],
        out_specs=row,
        out_shape=jax.ShapeDtypeStruct((n, d), F32),
        scratch_shapes=[pltpu.VMEM((2, TOP_K, tm * SUBLANES, LANES), F32), pltpu.SemaphoreType.DMA((2,))],
        compiler_params=_cparams(("arbitrary",)),
        name="combine",
    )(dest_t, dest_t, p, x, gate, g, b, ys)


def _moe(x, sc, sh, gate, ln_g, ln_b, w_router, b_router, w_gate_up, b_gate_up, w_down, b_down,
         *, layer, dn_alpha, per_row, tm, tiles_per_group, tmg):
    n, d = x.shape
    n_exp = w_router.shape[1]
    w_pad = jnp.pad(w_router, ((0, 0), (0, LANES - n_exp)))
    b_pad = jnp.pad(b_router, (0, LANES - n_exp)).reshape(1, LANES)
    idx, p, rank, cnt = _router(x, sc, sh, w_pad, b_pad, n_experts=n_exp, per_row=per_row, tm=tm,
                                tiles_per_group=tiles_per_group)
    counts = cnt[0, :n_exp].astype(jnp.int32)
    padded = ((counts + tmg - 1) // tmg) * tmg
    ends = jnp.cumsum(padded)
    offsets = ends - padded
    n_rows = ((n * TOP_K + n_exp * (tmg - 1)) // tmg) * tmg
    top_i = idx[:, :TOP_K]
    dest_t = (offsets[top_i] + rank[:, :TOP_K]).T
    n_tiles = (ends[-1] // tmg).astype(jnp.int32)
    tile_ids = jnp.minimum(jnp.arange(n_rows // tmg, dtype=jnp.int32), n_tiles - 1)
    tile_expert = jnp.sum((ends // tmg)[None, :] <= tile_ids[:, None], axis=1).astype(jnp.int32)
    tile_expert = jnp.minimum(tile_expert, n_exp - 1)

    xs = _dispatch(offsets + counts, padded - counts, dest_t, x, sc, sh, n_rows, per_row=per_row, tm=tm,
                   tiles_per_group=tiles_per_group)
    ys = _experts(tile_expert, n_tiles.reshape(1), xs, w_gate_up, b_gate_up, w_down, b_down, layer=layer, tmg=tmg)
    tmc = min(tm, 128)
    return _combine(dest_t, p, x, gate, ln_g, ln_b, ys, dn_alpha=dn_alpha, per_row=per_row, tm=tmc,
                    tiles_per_group=tiles_per_group * (tm // tmc))


def _expand_rows(q_flat, n_rows, group):
    d = q_flat.shape[1]
    r = lax.broadcasted_iota(jnp.int32, (n_rows, d), 0)
    c = lax.broadcasted_iota(jnp.int32, (n_rows, d), 1)
    q_rows = jnp.broadcast_to(q_flat.astype(F32), (n_rows, d))
    return jnp.where(c // group == r, q_rows, 0.0).astype(q_flat.dtype)


def _dec_a_kernel(pt_ref, q_ref, kn_ref, vn_ref, tabt_ref, lam_ref, g_ref, *rest,
                  lam_init, pages_per_step, page, past_len, n_heads):
    del pt_ref
    k_refs = rest[:pages_per_step]
    v_refs = rest[pages_per_step:2 * pages_per_step]
    o_ref = rest[2 * pages_per_step]
    m_scr, l_scr, acc_scr = rest[2 * pages_per_step + 1:]
    g = pl.program_id(1)
    ng = pl.num_programs(1)
    nr = 2 * n_heads
    d = q_ref.shape[1]
    dh = d // nr

    @pl.when(g == 0)
    def _():
        m_scr[...] = jnp.full_like(m_scr, NEG_BIG)
        l_scr[...] = jnp.zeros_like(l_scr)
        acc_scr[...] = jnp.zeros_like(acc_scr)

    qexp = _expand_rows(q_ref[...], nr, dh)
    tabt = tabt_ref[...]
    t1, t2, t3 = _split3(tabt)
    bk = lax.broadcasted_iota(jnp.int32, (LANES, page), 0)
    kl = lax.broadcasted_iota(jnp.int32, (1, page), 1)
    sr = lax.broadcasted_iota(jnp.int32, (page, page * n_heads), 0)
    scol = lax.broadcasted_iota(jnp.int32, (page, page * n_heads), 1)
    spread = (scol // n_heads == sr).astype(BF16)
    kr = lax.broadcasted_iota(jnp.int32, (nr, page * n_heads), 0)
    kc = lax.broadcasted_iota(jnp.int32, (nr, page * n_heads), 1)
    keep = (kc % n_heads) == (kr % n_heads)

    mx = m_scr[...]
    l = l_scr[...]
    acc = acc_scr[...]
    for j in range(pages_per_step):
        s = _dot(qexp, k_refs[j][...].astype(BF16))
        pos = (g * pages_per_step + j) * page + kl
        dist = past_len - pos
        bucket = _bias_from_dist(dist, lambda b: jnp.int32(b))
        onehot = (jnp.broadcast_to(bucket, (LANES, page)) == bk).astype(BF16)
        s = s + (_dot(t1, onehot) + _dot(t2, onehot) + _dot(t3, onehot))
        mx_new = jnp.maximum(mx, jnp.max(s, axis=1, keepdims=True))
        alpha = jnp.exp(mx - mx_new)
        p = jnp.exp(s - mx_new)
        l = alpha * l + jnp.sum(p, axis=1, keepdims=True)
        pe = jnp.where(keep, _dot(p.astype(BF16), spread), 0.0).astype(BF16)
        acc = alpha * acc + _dot(pe, v_refs[j][...].astype(BF16))
        mx = mx_new
    m_scr[...] = mx
    l_scr[...] = l
    acc_scr[...] = acc

    @pl.when(g == ng - 1)
    def _():
        kexp = jnp.broadcast_to(kn_ref[...], (nr, d)).astype(F32)
        s_new = jnp.sum(qexp.astype(F32) * kexp, axis=1, keepdims=True) + tabt[:, 0:1]
        mx2 = jnp.maximum(mx, s_new)
        alpha = jnp.exp(mx - mx2)
        p_new = jnp.exp(s_new - mx2)
        l2 = alpha * l + p_new
        vn = vn_ref[...].astype(F32)
        acc2 = alpha * acc + p_new * jnp.concatenate([vn, vn], axis=0)
        o = acc2 / l2
        lam = _diff_lambda(lam_ref, lam_init)
        o = o[:n_heads] - lam * o[n_heads:]
        ms = jnp.mean(o * o, axis=1, keepdims=True)
        o = (o * lax.rsqrt(ms + LN_EPS)) * g_ref[...] * (1.0 - lam_init)
        o_ref[...] = o.astype(BF16)


def _dec_a(page_table, qb, kb_new, vb_new, rel_tab, lam_vecs, subln_g, kt_pages, v_pages,
           *, layer, lam_init, n_heads, page, pages_per_step):
    s_n, n_pages = page_table.shape
    d = qb.shape[1]
    nr = 2 * n_heads
    dv = d // n_heads
    past_len = n_pages * page
    tabt = jnp.pad(rel_tab.reshape(N_BUCKETS, nr).T, ((0, 0), (0, LANES - N_BUCKETS)))
    pps = pages_per_step
    page_spec = lambda j: pl.BlockSpec((None, None, kt_pages.shape[2], page),
                                       lambda s, g, pt: (layer, pt[s, g * pps + j], 0, 0))
    kernel = functools.partial(_dec_a_kernel, lam_init=lam_init, pages_per_step=pps, page=page,
                               past_len=past_len, n_heads=n_heads)
    grid_spec = pltpu.PrefetchScalarGridSpec(
        num_scalar_prefetch=1,
        grid=(s_n, n_pages // pps),
        in_specs=[
            pl.BlockSpec((None, 1, d), lambda s, g, pt: (s, 0, 0)),
            pl.BlockSpec((None, 1, d), lambda s, g, pt: (s, 0, 0)),
            pl.BlockSpec((None, n_heads, dv), lambda s, g, pt: (s, 0, 0)),
            pl.BlockSpec((nr, LANES), lambda s, g, pt: (0, 0)),
            pl.BlockSpec((4, lam_vecs.shape[1]), lambda s, g, pt: (0, 0)),
            pl.BlockSpec((1, dv), lambda s, g, pt: (0, 0)),
        ] + [page_spec(j) for j in range(pps)] + [page_spec(j) for j in range(pps)],
        out_specs=pl.BlockSpec((None, n_heads, dv), lambda s, g, pt: (s, 0, 0)),
        scratch_shapes=[pltpu.VMEM((nr, 1), F32), pltpu.VMEM((nr, 1), F32), pltpu.VMEM((nr, dv), F32)],
    )
    out = pl.pallas_call(
        kernel,
        grid_spec=grid_spec,
        out_shape=jax.ShapeDtypeStruct((s_n, n_heads, dv), BF16),
        compiler_params=_cparams(("arbitrary", "arbitrary")),
        name="dec_a",
    )(page_table, qb.reshape(s_n, 1, d), kb_new.reshape(s_n, 1, d), vb_new.reshape(s_n, n_heads, dv),
      tabt, lam_vecs, subln_g, *([kt_pages] * pps), *([v_pages] * pps))
    return out.reshape(s_n, d)


def _dec_b_kernel(pt_ref, q_ref, k_hbm, v_hbm, o_ref, kbuf, vbuf, sems, *, layer, n_pages, page, n_heads):
    s = pl.program_id(0)
    d = q_ref.shape[1]
    dh = d // n_heads

    def copies(pg, slot):
        phys = pt_ref[s, pg]
        return (pltpu.make_async_copy(k_hbm.at[layer, phys], kbuf.at[slot], sems.at[0, slot]),
                pltpu.make_async_copy(v_hbm.at[layer, phys], vbuf.at[slot], sems.at[1, slot]))

    def start(pg, slot):
        for c in copies(pg, slot):
            c.start()

    def wait(pg, slot):
        for c in copies(pg, slot):
            c.wait()

    qexp = _expand_rows(q_ref[...], n_heads, dh)
    row = lax.broadcasted_iota(jnp.int32, (page, page), 0)
    col = lax.broadcasted_iota(jnp.int32, (page, page), 1)
    upper = (row > col).astype(BF16)

    start(n_pages - 1, 0)

    def live(c):
        pg, tot, _ = c
        return (pg >= 0) & (jnp.max(tot) > SB_DEAD)

    def visit(c):
        pg, tot, acc = c
        slot = (n_pages - 1 - pg) % 2
        wait(pg, slot)

        @pl.when(pg > 0)
        def _():
            start(pg - 1, 1 - slot)

        z = _dot(qexp, kbuf[slot].astype(BF16))
        sp = _softplus(z)
        ls = -sp
        w = jnp.exp(z - sp + (tot + _suffix_sums(ls, upper)))
        acc = acc + _dot_nt(w.astype(BF16), vbuf[slot].astype(BF16))
        tot = tot + jnp.sum(ls, axis=1, keepdims=True)
        return pg - 1, tot, acc

    pg, _, acc = lax.while_loop(live, visit, (jnp.int32(n_pages - 1), jnp.zeros((n_heads, 1), F32),
                                             jnp.zeros((n_heads, d), F32)))

    @pl.when(pg >= 0)
    def _():
        wait(pg, (n_pages - 1 - pg) % 2)

    r = lax.broadcasted_iota(jnp.int32, (n_heads, d), 0)
    c = lax.broadcasted_iota(jnp.int32, (n_heads, d), 1)
    o_ref[...] = jnp.sum(jnp.where(c // dh == r, acc, 0.0), axis=0, keepdims=True).astype(BF16)


def _dec_b(page_table, qb, kt_pages, vt_pages, *, layer, n_heads, page):
    s_n, n_pages = page_table.shape
    d = qb.shape[1]
    grid_spec = pltpu.PrefetchScalarGridSpec(
        num_scalar_prefetch=1,
        grid=(s_n,),
        in_specs=[pl.BlockSpec((None, 1, d), lambda s, pt: (s, 0, 0)),
                  pl.BlockSpec(memory_space=pl.ANY), pl.BlockSpec(memory_space=pl.ANY)],
        out_specs=pl.BlockSpec((None, 1, d), lambda s, pt: (s, 0, 0)),
        scratch_shapes=[pltpu.VMEM((2, d, page), F32), pltpu.VMEM((2, d, page), F32),
                        pltpu.SemaphoreType.DMA((2, 2))],
    )
    out = pl.pallas_call(
        functools.partial(_dec_b_kernel, layer=layer, n_pages=n_pages, page=page, n_heads=n_heads),
        grid_spec=grid_spec,
        out_shape=jax.ShapeDtypeStruct((s_n, 1, d), BF16),
        compiler_params=_cparams(("arbitrary",)),
        name="dec_b",
    )(page_table, qb.reshape(s_n, 1, d), kt_pages, vt_pages)
    return out.reshape(s_n, d)


def kernel(x_prompt, x_sample, c_prompt, c_sample, cache_a_k, cache_a_v, cache_b_k, cache_b_v, page_table, rel_table, w_qkv_a, w_o_a, lambda_a, subln_a, w_qkv_b, w_o_b, w_ada, b_ada, ln_g, ln_b, w_router, b_router, w_gate_up, b_gate_up, w_down, b_down):
    batch, seq, d = x_prompt.shape
    s_n = x_sample.shape[0]
    depth = w_ada.shape[0]
    a_heads = cache_a_k.shape[4]
    a_dh = cache_a_k.shape[5]
    b_heads = cache_b_k.shape[3]
    b_dh = cache_b_k.shape[4]
    page = cache_a_k.shape[2]
    n_pool = cache_a_k.shape[1]
    dn_alpha = (2 * depth) ** 0.25
    n_p = batch * seq

    tm = min(512, seq)
    tiles_per_batch = seq // tm
    t_attn = min(256, seq)
    pps = min(8, page_table.shape[1])

    mods = _ada(jnp.concatenate([c_prompt, c_sample], axis=0), w_ada, b_ada)
    mods = mods.reshape(depth, batch + s_n, 6, d)

    xp = x_prompt.reshape(n_p, d)
    xs = x_sample.reshape(s_n, d)
    outs_p = {"a_k": [], "a_v": [], "b_k": [], "b_v": []}
    outs_s = {"a_k": [], "a_v": [], "b_k": [], "b_v": []}

    kt_a = jnp.transpose(cache_a_k, (0, 1, 3, 4, 5, 2)).reshape(-1, n_pool, 2 * a_heads * a_dh, page)
    v_a = cache_a_v.reshape(-1, n_pool, page * a_heads, 2 * a_dh)
    kt_b = jnp.transpose(cache_b_k, (0, 1, 3, 4, 2)).reshape(-1, n_pool, b_heads * b_dh, page)
    vt_b = jnp.transpose(cache_b_v, (0, 1, 3, 4, 2)).reshape(-1, n_pool, b_heads * b_dh, page)

    for i in range(depth):
        j = i // 2
        mp = [mods[i, :batch, c].reshape(batch, 1, d) for c in range(6)]
        ms = [mods[i, batch:, c] for c in range(6)]
        kw_p = dict(per_row=False, tm=tm, tiles_per_group=tiles_per_batch)
        kw_s = dict(per_row=True, tm=s_n, tiles_per_group=1)
        if i % 2 == 0:
            lam_init = 0.8 - 0.6 * math.exp(-0.3 * i)
            w_bf = w_qkv_a[j].astype(BF16)
            q_scale = a_dh ** -0.5
            qb, kb, vb, kf, vf = _qkv(xp, mp[1], mp[0], w_bf, q_scale=q_scale, k_t=True, **kw_p)
            o_p = _attn_a(qb, kb, vb, rel_table, lambda_a[j], subln_a[j].reshape(1, -1), batch=batch, seq=seq,
                          n_heads=a_heads, lam_init=lam_init, t=t_attn)
            qs, ks, vs, ksf, vsf = _qkv(xs, ms[1], ms[0], w_bf, q_scale=q_scale, **kw_s)
            o_s = _dec_a(page_table, qs, ks, vs, rel_table, lambda_a[j], subln_a[j].reshape(1, -1),
                         kt_a, v_a, layer=j, lam_init=lam_init, n_heads=a_heads, page=page, pages_per_step=pps)
            w_o = w_o_a[j].astype(BF16)
            outs_p["a_k"].append(jnp.transpose(kf.reshape(batch, 2, a_heads, a_dh, seq), (0, 4, 1, 2, 3)))
            outs_p["a_v"].append(vf.reshape(batch, seq, a_heads, 2 * a_dh))
            outs_s["a_k"].append(ksf.reshape(s_n, 1, 2, a_heads, a_dh))
            outs_s["a_v"].append(vsf.reshape(s_n, 1, a_heads, 2 * a_dh))
        else:
            w_bf = w_qkv_b[j].astype(BF16)
            q_scale = b_dh ** -0.5
            qb, kb, vb, kf, vf = _qkv(xp, mp[1], mp[0], w_bf, q_scale=q_scale, k_t=True, v_t=True, **kw_p)
            o_p = _attn_b(qb, kb, vb, batch=batch, seq=seq, n_heads=b_heads, t=t_attn)
            qs, ks, vs, ksf, vsf = _qkv(xs, ms[1], ms[0], w_bf, q_scale=q_scale, **kw_s)
            o_s = _dec_b(page_table, qs, kt_b, vt_b, layer=j, n_heads=b_heads, page=page)
            w_o = w_o_b[j].astype(BF16)
            outs_p["b_k"].append(jnp.transpose(kf.reshape(batch, b_heads, b_dh, seq), (0, 3, 1, 2)))
            outs_p["b_v"].append(jnp.transpose(vf.reshape(batch, b_heads, b_dh, seq), (0, 3, 1, 2)))
            outs_s["b_k"].append(ksf.reshape(s_n, 1, b_heads, b_dh))
            outs_s["b_v"].append(vsf.reshape(s_n, 1, b_heads, b_dh))
        g0 = ln_g[i, 0].reshape(1, d)
        b0 = ln_b[i, 0].reshape(1, d)
        g1 = ln_g[i, 1].reshape(1, d)
        b1 = ln_b[i, 1].reshape(1, d)
        xp = _proj_ln(o_p, w_o, xp, mp[2], g0, b0, dn_alpha=dn_alpha, **kw_p)
        xs = _proj_ln(o_s, w_o, xs, ms[2], g0, b0, dn_alpha=dn_alpha, **kw_s)
        moe_args = (w_router[i], b_router[i], w_gate_up, b_gate_up, w_down, b_down)
        xp = _moe(xp, mp[4], mp[3], mp[5], g1, b1, *moe_args, layer=i, dn_alpha=dn_alpha, tmg=256, **kw_p)
        xs = _moe(xs, ms[4], ms[3], ms[5], g1, b1, *moe_args, layer=i, dn_alpha=dn_alpha, tmg=8, **kw_s)

    return (xp.reshape(batch, seq, d), xs.reshape(s_n, 1, d),
            jnp.stack(outs_p["a_k"]), jnp.stack(outs_p["a_v"]), jnp.stack(outs_p["b_k"]), jnp.stack(outs_p["b_v"]),
            jnp.stack(outs_s["a_k"]), jnp.stack(outs_s["a_v"]), jnp.stack(outs_s["b_k"]), jnp.stack(outs_s["b_v"]))
```

```python
import functools
import math

import jax
import jax.numpy as jnp
from jax import lax
from jax.experimental import pallas as pl
from jax.experimental.pallas import tpu as pltpu

N_BUCKETS = 32
MAX_DISTANCE = 128
TOP_K = 4
SWIGLU_LIMIT = 7.0
SWIGLU_ALPHA = 1.702
LN_EPS = 1e-5
NEG_BIG = -1e30
SB_DEAD = -120.0

LANES = 128
SUBLANES = 8
VMEM_LIMIT_BYTES = 56 * 1024 * 1024

F32 = jnp.float32
BF16 = jnp.bfloat16


def _bucket_lower_bounds():
    max_exact = N_BUCKETS // 2
    lo = list(range(max_exact))
    n = max_exact
    for b in range(max_exact, N_BUCKETS):
        while True:
            large = max_exact + int(math.log(n / max_exact) / math.log(MAX_DISTANCE / max_exact)
                                    * (N_BUCKETS - max_exact))
            if min(large, N_BUCKETS - 1) >= b:
                break
            n += 1
        lo.append(n)
    return lo


BUCKET_LO = _bucket_lower_bounds()


def _cparams(sem, vmem=VMEM_LIMIT_BYTES):
    return pltpu.CompilerParams(dimension_semantics=sem, vmem_limit_bytes=vmem)


def _split3(a):
    a1 = a.astype(BF16)
    r = a - a1.astype(F32)
    a2 = r.astype(BF16)
    a3 = (r - a2.astype(F32)).astype(BF16)
    return a1, a2, a3


def _dot(a, b):
    return jnp.dot(a, b, preferred_element_type=F32)


def _dot_nt(a, b):
    return lax.dot_general(a, b, (((1,), (1,)), ((), ())), preferred_element_type=F32)


def _dot_precise(a, b):
    a1, a2, a3 = _split3(a)
    b1, b2, b3 = _split3(b)
    out = _dot(a1, b1)
    out += _dot(a1, b2) + _dot(a2, b1)
    out += _dot(a1, b3) + _dot(a2, b2) + _dot(a3, b1)
    return out


def _mod_spec(per_row, tm, tiles_per_group):
    if per_row:
        return lambda d: pl.BlockSpec((tm, d), lambda i, *_: (i, 0))
    return lambda d: pl.BlockSpec((None, 1, d), lambda i, *_: (i // tiles_per_group, 0, 0))


def _ada_kernel(c_ref, w_ref, b_ref, o_ref):
    c = c_ref[...]
    s = c * jax.nn.sigmoid(c)
    o_ref[...] = _dot_precise(s, w_ref[...]) + b_ref[...]


def _ada(c_all, w_ada, b_ada):
    depth, d, n6 = w_ada.shape
    nb = c_all.shape[0]
    tn = 1024
    return pl.pallas_call(
        _ada_kernel,
        grid=(depth, n6 // tn),
        in_specs=[
            pl.BlockSpec((nb, d), lambda l, j: (0, 0)),
            pl.BlockSpec((None, d, tn), lambda l, j: (l, 0, j)),
            pl.BlockSpec((None, 1, tn), lambda l, j: (l, 0, j)),
        ],
        out_specs=pl.BlockSpec((None, nb, tn), lambda l, j: (l, 0, j)),
        out_shape=jax.ShapeDtypeStruct((depth, nb, n6), F32),
        compiler_params=_cparams(("arbitrary", "arbitrary")),
        name="ada",
    )(c_all, w_ada, b_ada.reshape(depth, 1, n6))


def _qkv_kernel(x_ref, sc_ref, sh_ref, w_ref, q_ref, kb_ref, vb_ref, k_ref, v_ref, *, q_scale, k_t, v_t):
    d = x_ref.shape[1]
    h = (x_ref[...] * (1.0 + sc_ref[...]) + sh_ref[...]).astype(BF16)
    q = _dot(h, w_ref[:, 0:d])
    q_ref[...] = (q * q_scale).astype(BF16)
    k = _dot(h, w_ref[:, d:2 * d])
    k_ref[...] = k.T if k_t else k
    kb_ref[...] = k.astype(BF16)
    v = _dot(h, w_ref[:, 2 * d:3 * d])
    v_ref[...] = v.T if v_t else v
    vb_ref[...] = v.astype(BF16)


def _qkv(x, sc, sh, w_bf, *, q_scale, per_row, tm, tiles_per_group, k_t=False, v_t=False):
    n, d = x.shape
    mod = _mod_spec(per_row, tm, tiles_per_group)
    row = pl.BlockSpec((tm, d), lambda i: (i, 0))
    tpg = tiles_per_group
    groups = n // (tm * tpg)
    t_spec = pl.BlockSpec((None, d, tm), lambda i: (i // tpg, 0, i % tpg))
    t_shape = jax.ShapeDtypeStruct((groups, d, tm * tpg), F32)
    n_shape = jax.ShapeDtypeStruct((n, d), F32)
    return pl.pallas_call(
        functools.partial(_qkv_kernel, q_scale=q_scale, k_t=k_t, v_t=v_t),
        grid=(n // tm,),
        in_specs=[row, mod(d), mod(d), pl.BlockSpec((d, 3 * d), lambda i: (0, 0))],
        out_specs=[row] * 3 + [t_spec if k_t else row, t_spec if v_t else row],
        out_shape=[jax.ShapeDtypeStruct((n, d), BF16)] * 3 + [t_shape if k_t else n_shape,
                                                             t_shape if v_t else n_shape],
        compiler_params=_cparams(("arbitrary",)),
        name="qkv",
    )(x, sc, sh, w_bf)


def _bias_from_dist(dist, table_value):
    val = table_value(N_BUCKETS - 1)
    for b in range(N_BUCKETS - 2, -1, -1):
        val = jnp.where(dist < BUCKET_LO[b + 1], table_value(b), val)
    return val


def _diff_lambda(lam_ref, lam_init):
    lv = lam_ref[...]
    a = jnp.sum(lv[0:1, :] * lv[1:2, :], axis=1, keepdims=True)
    b = jnp.sum(lv[2:3, :] * lv[3:4, :], axis=1, keepdims=True)
    return jnp.exp(a) - jnp.exp(b) + lam_init


def _attn_a_kernel(tab_ref, lam_ref, g_ref, q0_ref, q1_ref, k0_ref, k1_ref, v_ref, o_ref, bias_scr,
                   *, lam_init, t, n_heads):
    hp = pl.program_id(0)
    b = pl.program_id(1)
    qi = pl.program_id(2)
    dh2 = v_ref.shape[1] // 2
    assert t >= BUCKET_LO[N_BUCKETS - 1]

    @pl.when((b == 0) & (qi == 0))
    def _build_bias():
        row = lax.broadcasted_iota(jnp.int32, (t, t), 0)
        col = lax.broadcasted_iota(jnp.int32, (t, t), 1)
        for delta in range(2):
            dist = row - col + delta * t
            for m in range(2):
                for hh in range(2):
                    c = m * n_heads + 2 * hp + hh
                    far = tab_ref[N_BUCKETS - 1, c]
                    val = _bias_from_dist(dist, lambda bkt: tab_ref[bkt, c] - far)
                    if delta == 0:
                        val = jnp.where(dist < 0, NEG_BIG, val)
                    bias_scr[m, hh, delta] = val

    lam = _diff_lambda(lam_ref, lam_init)
    lane = lax.broadcasted_iota(jnp.int32, (t, LANES), 1)
    k_refs = (k0_ref, k1_ref)
    chains = [(m, hh) for hh in range(2) for m in range(2)]
    qm = {}
    for hh in range(2):
        in_head = (lane // (LANES // 2)) == hh
        for m, q_ref in enumerate((q0_ref, q1_ref)):
            qm[(m, hh)] = jnp.where(in_head, q_ref[...], jnp.zeros((), BF16))

    def step(kb, carry, with_bias):
        start = pl.multiple_of(kb * t, t)
        kts = [k_refs[m][pl.ds(start, t), :] for m in range(2)]
        vts = [v_ref[pl.ds(start, t), hh * dh2:(hh + 1) * dh2] for hh in range(2)]
        new = []
        for ci, (m, hh) in enumerate(chains):
            mx, l, acc = carry[ci]
            s = _dot_nt(qm[(m, hh)], kts[m])
            if with_bias:
                s = s + bias_scr[m, hh, qi - kb]
            mx_new = jnp.maximum(mx, jnp.max(s, axis=1, keepdims=True))
            alpha = jnp.exp(mx - mx_new)
            p = jnp.exp(s - mx_new)
            l = alpha * l + jnp.sum(p, axis=1, keepdims=True)
            acc = alpha * acc + _dot(p.astype(BF16), vts[hh])
            new.append((mx_new, l, acc))
        return tuple(new)

    init = tuple((jnp.full((t, 1), NEG_BIG, F32), jnp.zeros((t, 1), F32), jnp.zeros((t, dh2), F32))
                 for _ in chains)
    n_far = jnp.maximum(qi - 1, 0)
    carry = lax.fori_loop(0, n_far, lambda kb, c: step(kb, c, False), init)
    carry = lax.fori_loop(n_far, qi + 1, lambda kb, c: step(kb, c, True), carry)
    for hh in range(2):
        (_, l1, a1), (_, l2, a2) = carry[2 * hh], carry[2 * hh + 1]
        o = a1 / l1 - lam * (a2 / l2)
        ms = jnp.mean(o * o, axis=1, keepdims=True)
        o = (o * lax.rsqrt(ms + LN_EPS)) * g_ref[...] * (1.0 - lam_init)
        o_ref[:, hh * dh2:(hh + 1) * dh2] = o.astype(BF16)


def _attn_a(qb, kb, vb, rel_tab, lam_vecs, subln_g, *, batch, seq, n_heads, lam_init, t):
    n, d = qb.shape
    nq = seq // t
    npair = n_heads // 2
    tab2d = rel_tab.reshape(N_BUCKETS, 2 * n_heads)
    kernel = functools.partial(_attn_a_kernel, lam_init=lam_init, t=t, n_heads=n_heads)
    qspec = lambda off: pl.BlockSpec((t, LANES), lambda hp, b, qi: (b * nq + qi, off + hp))
    kspec = lambda off: pl.BlockSpec((seq, LANES), lambda hp, b, qi: (b, off + hp))
    return pl.pallas_call(
        kernel,
        grid=(npair, batch, nq),
        in_specs=[
            pl.BlockSpec(memory_space=pltpu.SMEM),
            pl.BlockSpec((4, lam_vecs.shape[1]), lambda hp, b, qi: (0, 0)),
            pl.BlockSpec((1, subln_g.shape[1]), lambda hp, b, qi: (0, 0)),
            qspec(0), qspec(npair), kspec(0), kspec(npair),
            pl.BlockSpec((seq, 2 * LANES), lambda hp, b, qi: (b, hp)),
        ],
        out_specs=pl.BlockSpec((t, 2 * LANES), lambda hp, b, qi: (b * nq + qi, hp)),
        out_shape=jax.ShapeDtypeStruct((n, d), BF16),
        scratch_shapes=[pltpu.VMEM((2, 2, 2, t, t), F32)],
        compiler_params=_cparams(("arbitrary", "arbitrary", "arbitrary")),
        name="attn_a",
    )(tab2d, lam_vecs, subln_g, qb, qb, kb, kb, vb)


def _softplus(z):
    return jnp.maximum(z, 0.0) + jnp.log(1.0 + jnp.exp(-jnp.abs(z)))


def _suffix_sums(ls, upper):
    hi = ls.astype(BF16)
    lo = (ls - hi.astype(F32)).astype(BF16)
    return _dot(hi, upper) + _dot(lo, upper)


def _attn_b_kernel(q_ref, k_ref, v_ref, o_ref, *, t):
    qi = pl.program_id(2)
    lane = lax.broadcasted_iota(jnp.int32, (t, LANES), 1)
    row = lax.broadcasted_iota(jnp.int32, (t, t), 0)
    col = lax.broadcasted_iota(jnp.int32, (t, t), 1)
    upper = (row > col).astype(BF16)
    below_diag = col < row
    q = q_ref[...]
    qm = [jnp.where((lane // (LANES // 2)) == hh, q, jnp.zeros((), BF16)) for hh in range(2)]

    def block_pair(kb, state, diagonal):
        has_prev = kb >= 1
        starts = (pl.multiple_of(kb * t, t), pl.multiple_of(jnp.maximum(kb - 1, 0) * t, t))
        kts = [k_ref[pl.ds(s, t), :] for s in starts]
        vts = [v_ref[pl.ds(s, t), :] for s in starts]
        new = []
        for hh in range(2):
            tot, acc = state[hh]
            z0 = _dot_nt(qm[hh], kts[0])
            z1 = _dot_nt(qm[hh], kts[1])
            sp0 = _softplus(z0)
            sp1 = _softplus(z1)
            ls0 = jnp.where(below_diag, -sp0, 0.0) if diagonal else -sp0
            ls1 = jnp.where(has_prev, -sp1, 0.0)
            tot0 = tot + jnp.sum(ls0, axis=1, keepdims=True)
            w0 = jnp.exp(z0 - sp0 + (tot + _suffix_sums(ls0, upper)))
            if diagonal:
                w0 = jnp.where(below_diag, w0, 0.0)
            w1 = jnp.where(has_prev, jnp.exp(z1 - sp1 + (tot0 + _suffix_sums(ls1, upper))), 0.0)
            acc = acc + _dot(w0.astype(BF16), vts[0]) + _dot(w1.astype(BF16), vts[1])
            new.append((tot0 + jnp.sum(ls1, axis=1, keepdims=True), acc))
        return tuple(new)

    zero = tuple((jnp.zeros((t, 1), F32), jnp.zeros((t, LANES), F32)) for _ in range(2))
    state = block_pair(qi, zero, True)

    def live(c):
        kb, st = c
        return (kb >= 0) & (jnp.max(jnp.maximum(st[0][0], st[1][0])) > SB_DEAD)

    def older(c):
        kb, st = c
        return kb - 2, block_pair(kb, st, False)

    _, state = lax.while_loop(live, older, (qi - 2, state))
    o_ref[...] = jnp.where(lane < LANES // 2, state[0][1], state[1][1]).astype(BF16)


def _attn_b(qb, kb, vb, *, batch, seq, n_heads, t):
    n, d = qb.shape
    nq = seq // t
    npair = n_heads // 2
    tile = pl.BlockSpec((t, LANES), lambda hp, b, qi: (b * nq + qi, hp))
    full = pl.BlockSpec((seq, LANES), lambda hp, b, qi: (b, hp))
    return pl.pallas_call(
        functools.partial(_attn_b_kernel, t=t),
        grid=(npair, batch, nq),
        in_specs=[tile, full, full],
        out_specs=tile,
        out_shape=jax.ShapeDtypeStruct((n, d), BF16),
        compiler_params=_cparams(("arbitrary", "arbitrary", "arbitrary")),
        name="attn_b",
    )(qb, kb, vb)


def _layer_norm(z, g, b):
    mu = jnp.mean(z, axis=1, keepdims=True)
    zc = z - mu
    var = jnp.mean(zc * zc, axis=1, keepdims=True)
    return (zc * lax.rsqrt(var + LN_EPS)) * g + b


def _proj_ln_kernel(o_ref, w_ref, x_ref, gate_ref, g_ref, b_ref, out_ref, *, dn_alpha):
    y = _dot(o_ref[...], w_ref[...])
    z = dn_alpha * x_ref[...] + gate_ref[...] * y
    out_ref[...] = _layer_norm(z, g_ref[...], b_ref[...])


def _proj_ln(o, w_bf, x, gate, g, b, *, dn_alpha, per_row, tm, tiles_per_group):
    n, d = x.shape
    mod = _mod_spec(per_row, tm, tiles_per_group)
    row = pl.BlockSpec((tm, d), lambda i: (i, 0))
    vec = pl.BlockSpec((1, d), lambda i: (0, 0))
    return pl.pallas_call(
        functools.partial(_proj_ln_kernel, dn_alpha=dn_alpha),
        grid=(n // tm,),
        in_specs=[row, pl.BlockSpec((d, d), lambda i: (0, 0)), row, mod(d), vec, vec],
        out_specs=row,
        out_shape=jax.ShapeDtypeStruct((n, d), F32),
        compiler_params=_cparams(("arbitrary",)),
        name="proj_ln",
    )(o, w_bf, x, gate, g, b)


def _router_kernel(x_ref, sc_ref, sh_ref, w_ref, b_ref, idx_ref, p_ref, rank_ref, cnt_ref, carry_scr,
                   *, n_experts):
    i = pl.program_id(0)
    tm = x_ref.shape[0]

    @pl.when(i == 0)
    def _():
        carry_scr[...] = jnp.zeros_like(carry_scr)

    h = x_ref[...] * (1.0 + sc_ref[...]) + sh_ref[...]
    logits = _dot_precise(h, w_ref[...]) + b_ref[...]
    lane = lax.broadcasted_iota(jnp.int32, (tm, LANES), 1)
    lane_f = lane.astype(F32)
    l = jnp.where(lane < n_experts, logits, -jnp.inf)
    idx_out = jnp.zeros((tm, LANES), jnp.int32)
    top_v = []
    top_i = []
    for k in range(TOP_K):
        mk = jnp.max(l, axis=1, keepdims=True)
        ik = jnp.min(jnp.where(l == mk, lane_f, float(LANES)), axis=1, keepdims=True).astype(jnp.int32)
        top_v.append(mk)
        top_i.append(ik)
        idx_out = jnp.where(lane == k, ik, idx_out)
        l = jnp.where(lane == ik, -jnp.inf, l)
    es = [jnp.exp(v - top_v[0]) for v in top_v]
    den = es[0] + es[1] + es[2] + es[3]
    p_out = jnp.zeros((tm, LANES), F32)
    for k in range(TOP_K):
        p_out = jnp.where(lane == k, es[k] / den, p_out)

    onehot = jnp.zeros((tm, LANES), F32)
    for k in range(TOP_K):
        onehot = onehot + (lane == top_i[k]).astype(F32)
    r = lax.broadcasted_iota(jnp.int32, (tm, tm), 0)
    c = lax.broadcasted_iota(jnp.int32, (tm, tm), 1)
    lower = (c < r).astype(BF16)
    before = _dot(lower, onehot.astype(BF16)) + carry_scr[0:1, :]
    rank_out = jnp.zeros((tm, LANES), jnp.int32)
    for k in range(TOP_K):
        rk = jnp.sum(jnp.where(lane == top_i[k], before, 0.0), axis=1, keepdims=True)
        rank_out = jnp.where(lane == k, rk.astype(jnp.int32), rank_out)
    total = carry_scr[0:1, :] + jnp.sum(onehot, axis=0, keepdims=True)
    carry_scr[...] = jnp.broadcast_to(total, carry_scr.shape)
    idx_ref[...] = idx_out
    p_ref[...] = p_out
    rank_ref[...] = rank_out
    cnt_ref[...] = jnp.broadcast_to(total, cnt_ref.shape)


def _router(x, sc, sh, w_pad, b_pad, *, n_experts, per_row, tm, tiles_per_group):
    n, d = x.shape
    mod = _mod_spec(per_row, tm, tiles_per_group)
    row = pl.BlockSpec((tm, d), lambda i: (i, 0))
    lrow = pl.BlockSpec((tm, LANES), lambda i: (i, 0))
    return pl.pallas_call(
        functools.partial(_router_kernel, n_experts=n_experts),
        grid=(n // tm,),
        in_specs=[row, mod(d), mod(d), pl.BlockSpec((d, LANES), lambda i: (0, 0)),
                  pl.BlockSpec((1, LANES), lambda i: (0, 0))],
        out_specs=[lrow, lrow, lrow, pl.BlockSpec((SUBLANES, LANES), lambda i: (0, 0))],
        out_shape=[jax.ShapeDtypeStruct((n, LANES), jnp.int32), jax.ShapeDtypeStruct((n, LANES), F32),
                   jax.ShapeDtypeStruct((n, LANES), jnp.int32),
                   jax.ShapeDtypeStruct((SUBLANES, LANES), F32)],
        scratch_shapes=[pltpu.VMEM((SUBLANES, LANES), F32)],
        compiler_params=_cparams(("arbitrary",)),
        name="router",
    )(x, sc, sh, w_pad, b_pad)


def _slab_rows(d):
    assert d % LANES == 0 and d // LANES == SUBLANES
    return d // LANES


def _store_slabs(ref, val, tm):
    for c in range(SUBLANES):
        ref[pl.ds(c, tm, stride=SUBLANES), :] = val[:, c * LANES:(c + 1) * LANES]


def _load_slabs(ref, tm, lead=()):
    return [ref[lead + (pl.ds(c, tm, stride=SUBLANES), slice(None))] for c in range(SUBLANES)]


def _dispatch_kernel(pad_start_ref, pad_len_ref, dest_ref, x_ref, sc_ref, sh_ref, xs_ref, hbuf, zbuf, sem, zsem,
                     *, n_experts):
    i = pl.program_id(0)
    tm = x_ref.shape[0]

    def zero_copy(e, j):
        row = pl.multiple_of((pad_start_ref[e] + j) * SUBLANES, SUBLANES)
        return pltpu.make_async_copy(zbuf, xs_ref.at[pl.ds(row, SUBLANES)], zsem)

    @pl.when(i == 0)
    def _():
        zbuf[...] = jnp.zeros_like(zbuf)
        for e in range(n_experts):
            lax.fori_loop(0, pad_len_ref[e], lambda j, _: (zero_copy(e, j).start(), 0)[1], 0)
        for e in range(n_experts):
            lax.fori_loop(0, pad_len_ref[e], lambda j, _: (zero_copy(e, j).wait(), 0)[1], 0)

    _store_slabs(hbuf, x_ref[...] * (1.0 + sc_ref[...]) + sh_ref[...], tm)

    def row_copy(r, k):
        src = pl.multiple_of(r * SUBLANES, SUBLANES)
        dst = pl.multiple_of(dest_ref[k * tm + r] * SUBLANES, SUBLANES)
        return pltpu.make_async_copy(hbuf.at[pl.ds(src, SUBLANES)], xs_ref.at[pl.ds(dst, SUBLANES)], sem)

    def start(r, _):
        for k in range(TOP_K):
            row_copy(r, k).start(priority=k % 2)
        return 0

    def wait(r, _):
        for k in range(TOP_K):
            row_copy(r, k).wait()
        return 0

    lax.fori_loop(0, tm, start, 0, unroll=8)
    lax.fori_loop(0, tm, wait, 0, unroll=8)


def _tile_major(dest, tm):
    n, k = dest.shape
    return dest.reshape(n // tm, tm, k).transpose(0, 2, 1).reshape(-1)


def _dispatch(pad_start, pad_len, dest, x, sc, sh, n_rows, *, per_row, tm, tiles_per_group):
    n, d = x.shape
    _slab_rows(d)
    mod = _mod_spec(per_row, tm, tiles_per_group)
    grid_spec = pltpu.PrefetchScalarGridSpec(
        num_scalar_prefetch=2,
        grid=(n // tm,),
        in_specs=[
            pl.BlockSpec((TOP_K * tm,), lambda i, *_: (i,), memory_space=pltpu.SMEM),
            pl.BlockSpec((tm, d), lambda i, *_: (i, 0)), mod(d), mod(d),
        ],
        out_specs=pl.BlockSpec(memory_space=pl.ANY),
        scratch_shapes=[pltpu.VMEM((tm * SUBLANES, LANES), F32), pltpu.VMEM((SUBLANES, LANES), F32),
                        pltpu.SemaphoreType.DMA, pltpu.SemaphoreType.DMA],
    )
    return pl.pallas_call(
        functools.partial(_dispatch_kernel, n_experts=pad_start.shape[0]),
        grid_spec=grid_spec,
        out_shape=jax.ShapeDtypeStruct((n_rows * SUBLANES, LANES), F32),
        compiler_params=_cparams(("arbitrary",)),
        name="dispatch",
    )(pad_start, pad_len, _tile_major(dest, tm), x, sc, sh)


def _experts_kernel(te_ref, nt_ref, xs_ref, wgu_ref, bgu_ref, wd_ref, bd_ref, ys_ref, wgu_bf, wd_bf, *, tmg):
    t = pl.program_id(0)
    de = wd_ref.shape[0]

    @pl.when((t == 0) | (te_ref[t] != te_ref[jnp.maximum(t - 1, 0)]))
    def _():
        wgu_bf[...] = wgu_ref[...].astype(BF16)
        wd_bf[...] = wd_ref[...].astype(BF16)

    @pl.when(t < nt_ref[0])
    def _():
        x = jnp.concatenate([c.astype(BF16) for c in _load_slabs(xs_ref, tmg)], axis=1)
        gu = _dot(x, wgu_bf[...]) + bgu_ref[...]
        gate = jnp.minimum(gu[:, :de], SWIGLU_LIMIT)
        up = jnp.clip(gu[:, de:], -SWIGLU_LIMIT, SWIGLU_LIMIT)
        act = (up + 1.0) * (gate * jax.nn.sigmoid(SWIGLU_ALPHA * gate))
        _store_slabs(ys_ref, _dot(act.astype(BF16), wd_bf[...]) + bd_ref[...], tmg)

    @pl.when(t >= nt_ref[0])
    def _():
        ys_ref[...] = jnp.zeros_like(ys_ref)


def _experts(tile_expert, n_tiles, xs, w_gate_up, b_gate_up, w_down, b_down, *, layer, tmg):
    _, n_exp, d, de2 = w_gate_up.shape
    de = w_down.shape[2]
    n_rows = xs.shape[0] // SUBLANES
    depth = w_gate_up.shape[0]
    last = lambda t, nt: jnp.minimum(t, nt[0] - 1)
    grid_spec = pltpu.PrefetchScalarGridSpec(
        num_scalar_prefetch=2,
        grid=(n_rows // tmg,),
        in_specs=[
            pl.BlockSpec((tmg * SUBLANES, LANES), lambda t, te, nt: (last(t, nt), 0)),
            pl.BlockSpec((None, None, d, de2), lambda t, te, nt: (layer, te[t], 0, 0)),
            pl.BlockSpec((None, None, 1, de2), lambda t, te, nt: (layer, te[t], 0, 0)),
            pl.BlockSpec((None, None, de, d), lambda t, te, nt: (layer, te[t], 0, 0)),
            pl.BlockSpec((None, None, 1, d), lambda t, te, nt: (layer, te[t], 0, 0)),
        ],
        out_specs=pl.BlockSpec((tmg * SUBLANES, LANES), lambda t, te, nt: (t, 0)),
        scratch_shapes=[pltpu.VMEM((d, de2), BF16), pltpu.VMEM((de, d), BF16)],
    )
    return pl.pallas_call(
        functools.partial(_experts_kernel, tmg=tmg),
        grid_spec=grid_spec,
        out_shape=jax.ShapeDtypeStruct((n_rows * SUBLANES, LANES), F32),
        compiler_params=_cparams(("arbitrary",)),
        name="experts",
    )(tile_expert, n_tiles, xs, w_gate_up, b_gate_up.reshape(depth, n_exp, 1, de2), w_down,
      b_down.reshape(depth, n_exp, 1, d))


def _combine_kernel(dest_ref, dest_next_ref, p_ref, x_ref, gate_ref, g_ref, b_ref, ys_ref, out_ref,
                    buf, sems, *, dn_alpha):
    i = pl.program_id(0)
    n = pl.num_programs(0)
    tm = x_ref.shape[0]
    slot = i % 2

    def row_copy(dref, s, r, k):
        src = pl.multiple_of(dref[k * tm + r] * SUBLANES, SUBLANES)
        dst = pl.multiple_of(r * SUBLANES, SUBLANES)
        return pltpu.make_async_copy(ys_ref.at[pl.ds(src, SUBLANES)], buf.at[s, k, pl.ds(dst, SUBLANES)],
                                     sems.at[s])

    def start_tile(dref, s):
        def body(r, _):
            for k in range(TOP_K):
                row_copy(dref, s, r, k).start(priority=k % 2)
            return 0
        lax.fori_loop(0, tm, body, 0, unroll=8)

    @pl.when(i == 0)
    def _():
        start_tile(dest_ref, 0)

    @pl.when(i + 1 < n)
    def _():
        start_tile(dest_next_ref, 1 - slot)

    def wait_body(r, _):
        for k in range(TOP_K):
            row_copy(dest_ref, slot, r, k).wait()
        return 0

    lax.fori_loop(0, tm, wait_body, 0, unroll=8)

    p = p_ref[...]
    pk = [p[:, k:k + 1] for k in range(TOP_K)]
    chunks = [_load_slabs(buf, tm, lead=(slot, k)) for k in range(TOP_K)]
    y = jnp.concatenate([sum(pk[k] * chunks[k][c] for k in range(TOP_K)) for c in range(SUBLANES)], axis=1)
    z = dn_alpha * x_ref[...] + gate_ref[...] * y
    out_ref[...] = _layer_norm(z, g_ref[...], b_ref[...])


def _combine(dest, p, x, gate, g, b, ys, *, dn_alpha, per_row, tm, tiles_per_group):
    n, d = x.shape
    nt = n // tm
    dest_t = _tile_major(dest, tm)
    mod = _mod_spec(per_row, tm, tiles_per_group)
    row = pl.BlockSpec((tm, d), lambda i: (i, 0))
    vec = pl.BlockSpec((1, d), lambda i: (0, 0))
    return pl.pallas_call(
        functools.partial(_combine_kernel, dn_alpha=dn_alpha),
        grid=(nt,),
        in_specs=[
            pl.BlockSpec((TOP_K * tm,), lambda i: (i,), memory_space=pltpu.SMEM),
            pl.BlockSpec((TOP_K * tm,), lambda i: (jnp.minimum(i + 1, nt - 1),), memory_space=pltpu.SMEM),
            pl.BlockSpec((tm, LANES), lambda i: (i, 0)),
            row, mod(d), vec, vec,
            pl.BlockSpec(memory_space=pl.ANY),
        ],
        out_specs=row,
        out_shape=jax.ShapeDtypeStruct((n, d), F32),
        scratch_shapes=[pltpu.VMEM((2, TOP_K, tm * SUBLANES, LANES), F32), pltpu.SemaphoreType.DMA((2,))],
        compiler_params=_cparams(("arbitrary",)),
        name="combine",
    )(dest_t, dest_t, p, x, gate, g, b, ys)


def _moe(x, sc, sh, gate, ln_g, ln_b, w_router, b_router, w_gate_up, b_gate_up, w_down, b_down,
         *, layer, dn_alpha, per_row, tm, tiles_per_group, tmg):
    n, d = x.shape
    n_exp = w_router.shape[1]
    w_pad = jnp.pad(w_router, ((0, 0), (0, LANES - n_exp)))
    b_pad = jnp.pad(b_router, (0, LANES - n_exp)).reshape(1, LANES)
    idx, p, rank, cnt = _router(x, sc, sh, w_pad, b_pad, n_experts=n_exp, per_row=per_row, tm=tm,
                                tiles_per_group=tiles_per_group)
    counts = cnt[0, :n_exp].astype(jnp.int32)
    padded = ((counts + tmg - 1) // tmg) * tmg
    ends = jnp.cumsum(padded)
    offsets = ends - padded
    n_rows = ((n * TOP_K + n_exp * (tmg - 1)) // tmg) * tmg
    top_i = idx[:, :TOP_K]
    dest = offsets[top_i] + rank[:, :TOP_K]
    n_tiles = (ends[-1] // tmg).astype(jnp.int32)
    tile_ids = jnp.minimum(jnp.arange(n_rows // tmg, dtype=jnp.int32), n_tiles - 1)
    tile_expert = jnp.sum((ends // tmg)[None, :] <= tile_ids[:, None], axis=1).astype(jnp.int32)
    tile_expert = jnp.minimum(tile_expert, n_exp - 1)

    xs = _dispatch(offsets + counts, padded - counts, dest, x, sc, sh, n_rows, per_row=per_row, tm=tm,
                   tiles_per_group=tiles_per_group)
    ys = _experts(tile_expert, n_tiles.reshape(1), xs, w_gate_up, b_gate_up, w_down, b_down, layer=layer, tmg=tmg)
    tmc = min(tm, 128)
    return _combine(dest, p, x, gate, ln_g, ln_b, ys, dn_alpha=dn_alpha, per_row=per_row, tm=tmc,
                    tiles_per_group=tiles_per_group * (tm // tmc))


def _expand_rows(q_flat, n_rows, group):
    d = q_flat.shape[1]
    r = lax.broadcasted_iota(jnp.int32, (n_rows, d), 0)
    c = lax.broadcasted_iota(jnp.int32, (n_rows, d), 1)
    q_rows = jnp.broadcast_to(q_flat.astype(F32), (n_rows, d))
    return jnp.where(c // group == r, q_rows, 0.0).astype(q_flat.dtype)


def _dec_a_kernel(pt_ref, q_ref, kn_ref, vn_ref, tabt_ref, lam_ref, g_ref, *rest,
                  lam_init, pages_per_step, page, past_len, n_heads):
    del pt_ref
    k_refs = rest[:pages_per_step]
    v_refs = rest[pages_per_step:2 * pages_per_step]
    o_ref = rest[2 * pages_per_step]
    m_scr, l_scr, acc_scr = rest[2 * pages_per_step + 1:]
    g = pl.program_id(1)
    ng = pl.num_programs(1)
    nr = 2 * n_heads
    d = q_ref.shape[1]
    dh = d // nr

    @pl.when(g == 0)
    def _():
        m_scr[...] = jnp.full_like(m_scr, NEG_BIG)
        l_scr[...] = jnp.zeros_like(l_scr)
        acc_scr[...] = jnp.zeros_like(acc_scr)

    qexp = _expand_rows(q_ref[...], nr, dh)
    assert page >= BUCKET_LO[N_BUCKETS - 1]
    tabt = tabt_ref[...] - tabt_ref[:, N_BUCKETS - 1:N_BUCKETS]
    t1, t2, t3 = _split3(tabt)
    bk = lax.broadcasted_iota(jnp.int32, (LANES, page), 0)
    kl = lax.broadcasted_iota(jnp.int32, (1, page), 1)
    sr = lax.broadcasted_iota(jnp.int32, (page, page * n_heads), 0)
    scol = lax.broadcasted_iota(jnp.int32, (page, page * n_heads), 1)
    spread = (scol // n_heads == sr).astype(BF16)
    kr = lax.broadcasted_iota(jnp.int32, (nr, page * n_heads), 0)
    kc = lax.broadcasted_iota(jnp.int32, (nr, page * n_heads), 1)
    keep = (kc % n_heads) == (kr % n_heads)

    mx = m_scr[...]
    l = l_scr[...]
    acc = acc_scr[...]
    s_pages = [_dot(qexp, k_refs[j][...].astype(BF16)) for j in range(pages_per_step)]
    pos = (g * pages_per_step + pages_per_step - 1) * page + kl
    bucket = _bias_from_dist(past_len - pos, lambda b: jnp.int32(b))
    onehot = (jnp.broadcast_to(bucket, (LANES, page)) == bk).astype(BF16)
    s_pages[-1] = s_pages[-1] + (_dot(t1, onehot) + _dot(t2, onehot) + _dot(t3, onehot))
    s_all = jnp.concatenate(s_pages, axis=1)
    mx_new = jnp.maximum(mx, jnp.max(s_all, axis=1, keepdims=True))
    alpha = jnp.exp(mx - mx_new)
    p_all = jnp.exp(s_all - mx_new)
    l = alpha * l + jnp.sum(p_all, axis=1, keepdims=True)
    acc = alpha * acc
    for j in range(pages_per_step):
        p = p_all[:, j * page:(j + 1) * page].astype(BF16)
        pe = jnp.where(keep, _dot(p, spread), 0.0).astype(BF16)
        acc = acc + _dot(pe, v_refs[j][...].astype(BF16))
    mx = mx_new
    m_scr[...] = mx
    l_scr[...] = l
    acc_scr[...] = acc

    @pl.when(g == ng - 1)
    def _():
        kexp = jnp.broadcast_to(kn_ref[...], (nr, d)).astype(F32)
        s_new = jnp.sum(qexp.astype(F32) * kexp, axis=1, keepdims=True) + tabt[:, 0:1]
        mx2 = jnp.maximum(mx, s_new)
        alpha = jnp.exp(mx - mx2)
        p_new = jnp.exp(s_new - mx2)
        l2 = alpha * l + p_new
        vn = vn_ref[...].astype(F32)
        acc2 = alpha * acc + p_new * jnp.concatenate([vn, vn], axis=0)
        o = acc2 / l2
        lam = _diff_lambda(lam_ref, lam_init)
        o = o[:n_heads] - lam * o[n_heads:]
        ms = jnp.mean(o * o, axis=1, keepdims=True)
        o = (o * lax.rsqrt(ms + LN_EPS)) * g_ref[...] * (1.0 - lam_init)
        o_ref[...] = o.astype(BF16)


def _dec_a(page_table, qb, kb_new, vb_new, rel_tab, lam_vecs, subln_g, kt_pages, v_pages,
           *, layer, lam_init, n_heads, page, pages_per_step):
    s_n, n_pages = page_table.shape
    d = qb.shape[1]
    nr = 2 * n_heads
    dv = d // n_heads
    past_len = n_pages * page
    tabt = jnp.pad(rel_tab.reshape(N_BUCKETS, nr).T, ((0, 0), (0, LANES - N_BUCKETS)))
    pps = pages_per_step
    page_spec = lambda j: pl.BlockSpec((None, None, kt_pages.shape[2], page),
                                       lambda s, g, pt: (layer, pt[s, g * pps + j], 0, 0))
    kernel = functools.partial(_dec_a_kernel, lam_init=lam_init, pages_per_step=pps, page=page,
                               past_len=past_len, n_heads=n_heads)
    grid_spec = pltpu.PrefetchScalarGridSpec(
        num_scalar_prefetch=1,
        grid=(s_n, n_pages // pps),
        in_specs=[
            pl.BlockSpec((None, 1, d), lambda s, g, pt: (s, 0, 0)),
            pl.BlockSpec((None, 1, d), lambda s, g, pt: (s, 0, 0)),
            pl.BlockSpec((None, n_heads, dv), lambda s, g, pt: (s, 0, 0)),
            pl.BlockSpec((nr, LANES), lambda s, g, pt: (0, 0)),
            pl.BlockSpec((4, lam_vecs.shape[1]), lambda s, g, pt: (0, 0)),
            pl.BlockSpec((1, dv), lambda s, g, pt: (0, 0)),
        ] + [page_spec(j) for j in range(pps)] + [page_spec(j) for j in range(pps)],
        out_specs=pl.BlockSpec((None, n_heads, dv), lambda s, g, pt: (s, 0, 0)),
        scratch_shapes=[pltpu.VMEM((nr, 1), F32), pltpu.VMEM((nr, 1), F32), pltpu.VMEM((nr, dv), F32)],
    )
    out = pl.pallas_call(
        kernel,
        grid_spec=grid_spec,
        out_shape=jax.ShapeDtypeStruct((s_n, n_heads, dv), BF16),
        compiler_params=_cparams(("arbitrary", "arbitrary")),
        name="dec_a",
    )(page_table, qb.reshape(s_n, 1, d), kb_new.reshape(s_n, 1, d), vb_new.reshape(s_n, n_heads, dv),
      tabt, lam_vecs, subln_g, *([kt_pages] * pps), *([v_pages] * pps))
    return out.reshape(s_n, d)


def _dec_b_kernel(pt_ref, q_ref, k_hbm, v_hbm, o_ref, kbuf, vbuf, sems, *, layer, n_pages, page, n_heads):
    s = pl.program_id(0)
    d = q_ref.shape[1]
    dh = d // n_heads

    def copies(pg, slot):
        phys = pt_ref[s, pg]
        return (pltpu.make_async_copy(k_hbm.at[layer, phys], kbuf.at[slot], sems.at[0, slot]),
                pltpu.make_async_copy(v_hbm.at[layer, phys], vbuf.at[slot], sems.at[1, slot]))

    def start(pg, slot):
        for c in copies(pg, slot):
            c.start()

    def wait(pg, slot):
        for c in copies(pg, slot):
            c.wait()

    qexp = _expand_rows(q_ref[...], n_heads, dh)
    row = lax.broadcasted_iota(jnp.int32, (page, page), 0)
    col = lax.broadcasted_iota(jnp.int32, (page, page), 1)
    upper = (row > col).astype(BF16)

    start(n_pages - 1, 0)

    def live(c):
        pg, tot, _ = c
        return (pg >= 0) & (jnp.max(tot) > SB_DEAD)

    def visit(c):
        pg, tot, acc = c
        slot = (n_pages - 1 - pg) % 2
        wait(pg, slot)

        @pl.when(pg > 0)
        def _():
            start(pg - 1, 1 - slot)

        z = _dot(qexp, kbuf[slot].astype(BF16))
        sp = _softplus(z)
        ls = -sp
        w = jnp.exp(z - sp + (tot + _suffix_sums(ls, upper)))
        acc = acc + _dot_nt(w.astype(BF16), vbuf[slot].astype(BF16))
        tot = tot + jnp.sum(ls, axis=1, keepdims=True)
        return pg - 1, tot, acc

    pg, _, acc = lax.while_loop(live, visit, (jnp.int32(n_pages - 1), jnp.zeros((n_heads, 1), F32),
                                             jnp.zeros((n_heads, d), F32)))

    @pl.when(pg >= 0)
    def _():
        wait(pg, (n_pages - 1 - pg) % 2)

    r = lax.broadcasted_iota(jnp.int32, (n_heads, d), 0)
    c = lax.broadcasted_iota(jnp.int32, (n_heads, d), 1)
    o_ref[...] = jnp.sum(jnp.where(c // dh == r, acc, 0.0), axis=0, keepdims=True).astype(BF16)


def _dec_b(page_table, qb, kt_pages, vt_pages, *, layer, n_heads, page):
    s_n, n_pages = page_table.shape
    d = qb.shape[1]
    grid_spec = pltpu.PrefetchScalarGridSpec(
        num_scalar_prefetch=1,
        grid=(s_n,),
        in_specs=[pl.BlockSpec((None, 1, d), lambda s, pt: (s, 0, 0)),
                  pl.BlockSpec(memory_space=pl.ANY), pl.BlockSpec(memory_space=pl.ANY)],
        out_specs=pl.BlockSpec((None, 1, d), lambda s, pt: (s, 0, 0)),
        scratch_shapes=[pltpu.VMEM((2, d, page), F32), pltpu.VMEM((2, d, page), F32),
                        pltpu.SemaphoreType.DMA((2, 2))],
    )
    out = pl.pallas_call(
        functools.partial(_dec_b_kernel, layer=layer, n_pages=n_pages, page=page, n_heads=n_heads),
        grid_spec=grid_spec,
        out_shape=jax.ShapeDtypeStruct((s_n, 1, d), BF16),
        compiler_params=_cparams(("arbitrary",)),
        name="dec_b",
    )(page_table, qb.reshape(s_n, 1, d), kt_pages, vt_pages)
    return out.reshape(s_n, d)


def kernel(x_prompt, x_sample, c_prompt, c_sample, cache_a_k, cache_a_v, cache_b_k, cache_b_v, page_table, rel_table, w_qkv_a, w_o_a, lambda_a, subln_a, w_qkv_b, w_o_b, w_ada, b_ada, ln_g, ln_b, w_router, b_router, w_gate_up, b_gate_up, w_down, b_down):
    batch, seq, d = x_prompt.shape
    s_n = x_sample.shape[0]
    depth = w_ada.shape[0]
    a_heads = cache_a_k.shape[4]
    a_dh = cache_a_k.shape[5]
    b_heads = cache_b_k.shape[3]
    b_dh = cache_b_k.shape[4]
    page = cache_a_k.shape[2]
    n_pool = cache_a_k.shape[1]
    dn_alpha = (2 * depth) ** 0.25
    n_p = batch * seq

    tm = min(512, seq)
    tiles_per_batch = seq // tm
    t_attn = min(256, seq)
    pps = min(8, page_table.shape[1])

    mods = _ada(jnp.concatenate([c_prompt, c_sample], axis=0), w_ada, b_ada)
    mods = mods.reshape(depth, batch + s_n, 6, d)

    xp = x_prompt.reshape(n_p, d)
    xs = x_sample.reshape(s_n, d)
    outs_p = {"a_k": [], "a_v": [], "b_k": [], "b_v": []}
    outs_s = {"a_k": [], "a_v": [], "b_k": [], "b_v": []}

    kt_a = jnp.transpose(cache_a_k, (0, 1, 3, 4, 5, 2)).reshape(-1, n_pool, 2 * a_heads * a_dh, page)
    v_a = cache_a_v.reshape(-1, n_pool, page * a_heads, 2 * a_dh)
    kt_b = jnp.transpose(cache_b_k, (0, 1, 3, 4, 2)).reshape(-1, n_pool, b_heads * b_dh, page)
    vt_b = jnp.transpose(cache_b_v, (0, 1, 3, 4, 2)).reshape(-1, n_pool, b_heads * b_dh, page)

    for i in range(depth):
        j = i // 2
        mp = [mods[i, :batch, c].reshape(batch, 1, d) for c in range(6)]
        ms = [mods[i, batch:, c] for c in range(6)]
        kw_p = dict(per_row=False, tm=tm, tiles_per_group=tiles_per_batch)
        kw_s = dict(per_row=True, tm=s_n, tiles_per_group=1)
        if i % 2 == 0:
            lam_init = 0.8 - 0.6 * math.exp(-0.3 * i)
            w_bf = w_qkv_a[j].astype(BF16)
            q_scale = a_dh ** -0.5
            qb, kb, vb, kf, vf = _qkv(xp, mp[1], mp[0], w_bf, q_scale=q_scale, k_t=True, **kw_p)
            o_p = _attn_a(qb, kb, vb, rel_table, lambda_a[j], subln_a[j].reshape(1, -1), batch=batch, seq=seq,
                          n_heads=a_heads, lam_init=lam_init, t=t_attn)
            qs, ks, vs, ksf, vsf = _qkv(xs, ms[1], ms[0], w_bf, q_scale=q_scale, **kw_s)
            o_s = _dec_a(page_table, qs, ks, vs, rel_table, lambda_a[j], subln_a[j].reshape(1, -1),
                         kt_a, v_a, layer=j, lam_init=lam_init, n_heads=a_heads, page=page, pages_per_step=pps)
            w_o = w_o_a[j].astype(BF16)
            outs_p["a_k"].append(jnp.transpose(kf.reshape(batch, 2, a_heads, a_dh, seq), (0, 4, 1, 2, 3)))
            outs_p["a_v"].append(vf.reshape(batch, seq, a_heads, 2 * a_dh))
            outs_s["a_k"].append(ksf.reshape(s_n, 1, 2, a_heads, a_dh))
            outs_s["a_v"].append(vsf.reshape(s_n, 1, a_heads, 2 * a_dh))
        else:
            w_bf = w_qkv_b[j].astype(BF16)
            q_scale = b_dh ** -0.5
            qb, kb, vb, kf, vf = _qkv(xp, mp[1], mp[0], w_bf, q_scale=q_scale, k_t=True, v_t=True, **kw_p)
            o_p = _attn_b(qb, kb, vb, batch=batch, seq=seq, n_heads=b_heads, t=t_attn)
            qs, ks, vs, ksf, vsf = _qkv(xs, ms[1], ms[0], w_bf, q_scale=q_scale, **kw_s)
            o_s = _dec_b(page_table, qs, kt_b, vt_b, layer=j, n_heads=b_heads, page=page)
            w_o = w_o_b[j].astype(BF16)
            outs_p["b_k"].append(jnp.transpose(kf.reshape(batch, b_heads, b_dh, seq), (0, 3, 1, 2)))
            outs_p["b_v"].append(jnp.transpose(vf.reshape(batch, b_heads, b_dh, seq), (0, 3, 1, 2)))
            outs_s["b_k"].append(ksf.reshape(s_n, 1, b_heads, b_dh))
            outs_s["b_v"].append(vsf.reshape(s_n, 1, b_heads, b_dh))
        g0 = ln_g[i, 0].reshape(1, d)
        b0 = ln_b[i, 0].reshape(1, d)
        g1 = ln_g[i, 1].reshape(1, d)
        b1 = ln_b[i, 1].reshape(1, d)
        xp = _proj_ln(o_p, w_o, xp, mp[2], g0, b0, dn_alpha=dn_alpha, **kw_p)
        xs = _proj_ln(o_s, w_o, xs, ms[2], g0, b0, dn_alpha=dn_alpha, **kw_s)
        moe_args = (w_router[i], b_router[i], w_gate_up, b_gate_up, w_down, b_down)
        xp = _moe(xp, mp[4], mp[3], mp[5], g1, b1, *moe_args, layer=i, dn_alpha=dn_alpha, tmg=512, **kw_p)
        xs = _moe(xs, ms[4], ms[3], ms[5], g1, b1, *moe_args, layer=i, dn_alpha=dn_alpha, tmg=8, **kw_s)

    return (xp.reshape(batch, seq, d), xs.reshape(s_n, 1, d),
            jnp.stack(outs_p["a_k"]), jnp.stack(outs_p["a_v"]), jnp.stack(outs_p["b_k"]), jnp.stack(outs_p["b_v"]),
            jnp.stack(outs_s["a_k"]), jnp.stack(outs_s["a_v"]), jnp.stack(outs_s["b_k"]), jnp.stack(outs_s["b_v"]))
```

```python
import functools
import math

import jax
import jax.numpy as jnp
from jax import lax
from jax.experimental import pallas as pl
from jax.experimental.pallas import tpu as pltpu

N_BUCKETS = 32
MAX_DISTANCE = 128
TOP_K = 4
SWIGLU_LIMIT = 7.0
SWIGLU_ALPHA = 1.702
LN_EPS = 1e-5
NEG_BIG = -1e30
SB_DEAD = -120.0

LANES = 128
SUBLANES = 8
VMEM_LIMIT_BYTES = 56 * 1024 * 1024

F32 = jnp.float32
BF16 = jnp.bfloat16


def _bucket_lower_bounds():
    max_exact = N_BUCKETS // 2
    lo = list(range(max_exact))
    n = max_exact
    for b in range(max_exact, N_BUCKETS):
        while True:
            large = max_exact + int(math.log(n / max_exact) / math.log(MAX_DISTANCE / max_exact)
                                    * (N_BUCKETS - max_exact))
            if min(large, N_BUCKETS - 1) >= b:
                break
            n += 1
        lo.append(n)
    return lo


BUCKET_LO = _bucket_lower_bounds()


def _cparams(sem, vmem=VMEM_LIMIT_BYTES):
    return pltpu.CompilerParams(dimension_semantics=sem, vmem_limit_bytes=vmem)


def _split3(a):
    a1 = a.astype(BF16)
    r = a - a1.astype(F32)
    a2 = r.astype(BF16)
    a3 = (r - a2.astype(F32)).astype(BF16)
    return a1, a2, a3


def _dot(a, b):
    return jnp.dot(a, b, preferred_element_type=F32)


def _dot_nt(a, b):
    return lax.dot_general(a, b, (((1,), (1,)), ((), ())), preferred_element_type=F32)


def _dot_precise(a, b):
    a1, a2, a3 = _split3(a)
    b1, b2, b3 = _split3(b)
    out = _dot(a1, b1)
    out += _dot(a1, b2) + _dot(a2, b1)
    out += _dot(a1, b3) + _dot(a2, b2) + _dot(a3, b1)
    return out


def _dot_3pass(a, b):
    a1 = a.astype(BF16)
    a2 = (a - a1.astype(F32)).astype(BF16)
    b1 = b.astype(BF16)
    b2 = (b - b1.astype(F32)).astype(BF16)
    return _dot(a1, b1) + (_dot(a1, b2) + _dot(a2, b1))


def _mod_spec(per_row, tm, tiles_per_group):
    if per_row:
        return lambda d: pl.BlockSpec((tm, d), lambda i, *_: (i, 0))
    return lambda d: pl.BlockSpec((None, 1, d), lambda i, *_: (i // tiles_per_group, 0, 0))


def _ada_kernel(c_ref, w_ref, b_ref, o_ref):
    c = c_ref[...]
    s = c * jax.nn.sigmoid(c)
    o_ref[...] = _dot_precise(s, w_ref[...]) + b_ref[...]


def _ada(c_all, w_ada, b_ada):
    depth, d, n6 = w_ada.shape
    nb = c_all.shape[0]
    tn = 1024
    return pl.pallas_call(
        _ada_kernel,
        grid=(depth, n6 // tn),
        in_specs=[
            pl.BlockSpec((nb, d), lambda l, j: (0, 0)),
            pl.BlockSpec((None, d, tn), lambda l, j: (l, 0, j)),
            pl.BlockSpec((None, 1, tn), lambda l, j: (l, 0, j)),
        ],
        out_specs=pl.BlockSpec((None, nb, tn), lambda l, j: (l, 0, j)),
        out_shape=jax.ShapeDtypeStruct((depth, nb, n6), F32),
        compiler_params=_cparams(("arbitrary", "arbitrary")),
        name="ada",
    )(c_all, w_ada, b_ada.reshape(depth, 1, n6))


def _qkv_kernel(x_ref, sc_ref, sh_ref, w_ref, q_ref, kb_ref, vb_ref, k_ref, v_ref, *, q_scale, k_t, v_t):
    d = x_ref.shape[1]
    h = (x_ref[...] * (1.0 + sc_ref[...]) + sh_ref[...]).astype(BF16)
    q = _dot(h, w_ref[:, 0:d])
    q_ref[...] = (q * q_scale).astype(BF16)
    k = _dot(h, w_ref[:, d:2 * d])
    k_ref[...] = k.T if k_t else k
    kb_ref[...] = k.astype(BF16)
    v = _dot(h, w_ref[:, 2 * d:3 * d])
    v_ref[...] = v.T if v_t else v
    vb_ref[...] = v.astype(BF16)


def _qkv(x, sc, sh, w_bf, *, q_scale, per_row, tm, tiles_per_group, k_t=False, v_t=False):
    n, d = x.shape
    mod = _mod_spec(per_row, tm, tiles_per_group)
    row = pl.BlockSpec((tm, d), lambda i: (i, 0))
    tpg = tiles_per_group
    groups = n // (tm * tpg)
    t_spec = pl.BlockSpec((None, d, tm), lambda i: (i // tpg, 0, i % tpg))
    t_shape = jax.ShapeDtypeStruct((groups, d, tm * tpg), F32)
    n_shape = jax.ShapeDtypeStruct((n, d), F32)
    return pl.pallas_call(
        functools.partial(_qkv_kernel, q_scale=q_scale, k_t=k_t, v_t=v_t),
        grid=(n // tm,),
        in_specs=[row, mod(d), mod(d), pl.BlockSpec((d, 3 * d), lambda i: (0, 0))],
        out_specs=[row] * 3 + [t_spec if k_t else row, t_spec if v_t else row],
        out_shape=[jax.ShapeDtypeStruct((n, d), BF16)] * 3 + [t_shape if k_t else n_shape,
                                                             t_shape if v_t else n_shape],
        compiler_params=_cparams(("arbitrary",)),
        name="qkv",
    )(x, sc, sh, w_bf)


def _bias_from_dist(dist, table_value):
    val = table_value(N_BUCKETS - 1)
    for b in range(N_BUCKETS - 2, -1, -1):
        val = jnp.where(dist < BUCKET_LO[b + 1], table_value(b), val)
    return val


def _diff_lambda(lam_ref, lam_init):
    lv = lam_ref[...]
    a = jnp.sum(lv[0:1, :] * lv[1:2, :], axis=1, keepdims=True)
    b = jnp.sum(lv[2:3, :] * lv[3:4, :], axis=1, keepdims=True)
    return jnp.exp(a) - jnp.exp(b) + lam_init


def _attn_a_kernel(tab_ref, lam_ref, g_ref, q0_ref, q1_ref, k0_ref, k1_ref, v_ref, o_ref, bias_scr,
                   *, lam_init, t, n_heads):
    hp = pl.program_id(0)
    b = pl.program_id(1)
    qi = pl.program_id(2)
    dh2 = v_ref.shape[1] // 2
    assert t >= BUCKET_LO[N_BUCKETS - 1]

    @pl.when((b == 0) & (qi == 0))
    def _build_bias():
        row = lax.broadcasted_iota(jnp.int32, (t, t), 0)
        col = lax.broadcasted_iota(jnp.int32, (t, t), 1)
        for delta in range(2):
            dist = row - col + delta * t
            for m in range(2):
                for hh in range(2):
                    c = m * n_heads + 2 * hp + hh
                    far = tab_ref[N_BUCKETS - 1, c]
                    val = _bias_from_dist(dist, lambda bkt: tab_ref[bkt, c] - far)
                    if delta == 0:
                        val = jnp.where(dist < 0, NEG_BIG, val)
                    bias_scr[m, hh, delta] = val

    lam = _diff_lambda(lam_ref, lam_init)
    lane = lax.broadcasted_iota(jnp.int32, (t, LANES), 1)
    k_refs = (k0_ref, k1_ref)
    chains = [(m, hh) for hh in range(2) for m in range(2)]
    qm = {}
    for hh in range(2):
        in_head = (lane // (LANES // 2)) == hh
        for m, q_ref in enumerate((q0_ref, q1_ref)):
            qm[(m, hh)] = jnp.where(in_head, q_ref[...], jnp.zeros((), BF16))

    def step(kb, carry, with_bias):
        start = pl.multiple_of(kb * t, t)
        kts = [k_refs[m][pl.ds(start, t), :] for m in range(2)]
        vts = [v_ref[pl.ds(start, t), hh * dh2:(hh + 1) * dh2] for hh in range(2)]
        new = []
        for ci, (m, hh) in enumerate(chains):
            mx, l, acc = carry[ci]
            s = _dot_nt(qm[(m, hh)], kts[m])
            if with_bias:
                s = s + bias_scr[m, hh, qi - kb]
            mx_new = jnp.maximum(mx, jnp.max(s, axis=1, keepdims=True))
            alpha = jnp.exp(mx - mx_new)
            p = jnp.exp(s - mx_new)
            l = alpha * l + jnp.sum(p, axis=1, keepdims=True)
            acc = alpha * acc + _dot(p.astype(BF16), vts[hh])
            new.append((mx_new, l, acc))
        return tuple(new)

    init = tuple((jnp.full((t, 1), NEG_BIG, F32), jnp.zeros((t, 1), F32), jnp.zeros((t, dh2), F32))
                 for _ in chains)
    n_far = jnp.maximum(qi - 1, 0)
    n_pairs = n_far // 2
    carry = lax.fori_loop(0, n_pairs, lambda i, c: step(2 * i + 1, step(2 * i, c, False), False), init)
    carry = lax.fori_loop(2 * n_pairs, n_far, lambda kb, c: step(kb, c, False), carry)
    has_prev = jnp.minimum(qi, 1)
    carry = lax.fori_loop(0, has_prev, lambda _, c: step(qi, step(qi - 1, c, True), True), carry)
    carry = lax.fori_loop(has_prev, 1, lambda _, c: step(qi, c, True), carry)
    for hh in range(2):
        (_, l1, a1), (_, l2, a2) = carry[2 * hh], carry[2 * hh + 1]
        o = a1 / l1 - lam * (a2 / l2)
        ms = jnp.mean(o * o, axis=1, keepdims=True)
        o = (o * lax.rsqrt(ms + LN_EPS)) * g_ref[...] * (1.0 - lam_init)
        o_ref[:, hh * dh2:(hh + 1) * dh2] = o.astype(BF16)


def _attn_a(qb, kb, vb, rel_tab, lam_vecs, subln_g, *, batch, seq, n_heads, lam_init, t):
    n, d = qb.shape
    nq = seq // t
    npair = n_heads // 2
    tab2d = rel_tab.reshape(N_BUCKETS, 2 * n_heads)
    kernel = functools.partial(_attn_a_kernel, lam_init=lam_init, t=t, n_heads=n_heads)
    qspec = lambda off: pl.BlockSpec((t, LANES), lambda hp, b, qi: (b * nq + qi, off + hp))
    kspec = lambda off: pl.BlockSpec((seq, LANES), lambda hp, b, qi: (b, off + hp))
    return pl.pallas_call(
        kernel,
        grid=(npair, batch, nq),
        in_specs=[
            pl.BlockSpec(memory_space=pltpu.SMEM),
            pl.BlockSpec((4, lam_vecs.shape[1]), lambda hp, b, qi: (0, 0)),
            pl.BlockSpec((1, subln_g.shape[1]), lambda hp, b, qi: (0, 0)),
            qspec(0), qspec(npair), kspec(0), kspec(npair),
            pl.BlockSpec((seq, 2 * LANES), lambda hp, b, qi: (b, hp)),
        ],
        out_specs=pl.BlockSpec((t, 2 * LANES), lambda hp, b, qi: (b * nq + qi, hp)),
        out_shape=jax.ShapeDtypeStruct((n, d), BF16),
        scratch_shapes=[pltpu.VMEM((2, 2, 2, t, t), F32)],
        compiler_params=_cparams(("arbitrary", "arbitrary", "arbitrary")),
        name="attn_a",
    )(tab2d, lam_vecs, subln_g, qb, qb, kb, kb, vb)


def _softplus(z):
    return jnp.maximum(z, 0.0) + jnp.log(1.0 + jnp.exp(-jnp.abs(z)))


def _suffix_sums(ls, upper):
    hi = ls.astype(BF16)
    lo = (ls - hi.astype(F32)).astype(BF16)
    return _dot(hi, upper) + _dot(lo, upper)


def _attn_b_kernel(q_ref, k_ref, v_ref, o_ref, *, t):
    qi = pl.program_id(2)
    lane = lax.broadcasted_iota(jnp.int32, (t, LANES), 1)
    row = lax.broadcasted_iota(jnp.int32, (t, t), 0)
    col = lax.broadcasted_iota(jnp.int32, (t, t), 1)
    upper = (row > col).astype(BF16)
    below_diag = col < row
    q = q_ref[...]
    qm = [jnp.where((lane // (LANES // 2)) == hh, q, jnp.zeros((), BF16)) for hh in range(2)]

    def block_pair(kb, state, diagonal):
        has_prev = kb >= 1
        starts = (pl.multiple_of(kb * t, t), pl.multiple_of(jnp.maximum(kb - 1, 0) * t, t))
        kts = [k_ref[pl.ds(s, t), :] for s in starts]
        vts = [v_ref[pl.ds(s, t), :] for s in starts]
        new = []
        for hh in range(2):
            tot, acc = state[hh]
            z0 = _dot_nt(qm[hh], kts[0])
            z1 = _dot_nt(qm[hh], kts[1])
            sp0 = _softplus(z0)
            sp1 = _softplus(z1)
            ls0 = jnp.where(below_diag, -sp0, 0.0) if diagonal else -sp0
            ls1 = jnp.where(has_prev, -sp1, 0.0)
            tot0 = tot + jnp.sum(ls0, axis=1, keepdims=True)
            w0 = jnp.exp(z0 - sp0 + (tot + _suffix_sums(ls0, upper)))
            if diagonal:
                w0 = jnp.where(below_diag, w0, 0.0)
            w1 = jnp.where(has_prev, jnp.exp(z1 - sp1 + (tot0 + _suffix_sums(ls1, upper))), 0.0)
            acc = acc + _dot(w0.astype(BF16), vts[0]) + _dot(w1.astype(BF16), vts[1])
            new.append((tot0 + jnp.sum(ls1, axis=1, keepdims=True), acc))
        return tuple(new)

    zero = tuple((jnp.zeros((t, 1), F32), jnp.zeros((t, LANES), F32)) for _ in range(2))
    state = block_pair(qi, zero, True)

    def live(c):
        kb, st = c
        return (kb >= 0) & (jnp.max(jnp.maximum(st[0][0], st[1][0])) > SB_DEAD)

    def older(c):
        kb, st = c
        return kb - 2, block_pair(kb, st, False)

    _, state = lax.while_loop(live, older, (qi - 2, state))
    o_ref[...] = jnp.where(lane < LANES // 2, state[0][1], state[1][1]).astype(BF16)


def _attn_b(qb, kb, vb, *, batch, seq, n_heads, t):
    n, d = qb.shape
    nq = seq // t
    npair = n_heads // 2
    tile = pl.BlockSpec((t, LANES), lambda hp, b, qi: (b * nq + qi, hp))
    full = pl.BlockSpec((seq, LANES), lambda hp, b, qi: (b, hp))
    return pl.pallas_call(
        functools.partial(_attn_b_kernel, t=t),
        grid=(npair, batch, nq),
        in_specs=[tile, full, full],
        out_specs=tile,
        out_shape=jax.ShapeDtypeStruct((n, d), BF16),
        compiler_params=_cparams(("arbitrary", "arbitrary", "arbitrary")),
        name="attn_b",
    )(qb, kb, vb)


def _layer_norm(z, g, b):
    mu = jnp.mean(z, axis=1, keepdims=True)
    zc = z - mu
    var = jnp.mean(zc * zc, axis=1, keepdims=True)
    return (zc * lax.rsqrt(var + LN_EPS)) * g + b


def _proj_ln_kernel(o_ref, w_ref, x_ref, gate_ref, g_ref, b_ref, out_ref, *, dn_alpha):
    y = _dot(o_ref[...], w_ref[...])
    z = dn_alpha * x_ref[...] + gate_ref[...] * y
    out_ref[...] = _layer_norm(z, g_ref[...], b_ref[...])


def _proj_ln(o, w_bf, x, gate, g, b, *, dn_alpha, per_row, tm, tiles_per_group):
    n, d = x.shape
    mod = _mod_spec(per_row, tm, tiles_per_group)
    row = pl.BlockSpec((tm, d), lambda i: (i, 0))
    vec = pl.BlockSpec((1, d), lambda i: (0, 0))
    return pl.pallas_call(
        functools.partial(_proj_ln_kernel, dn_alpha=dn_alpha),
        grid=(n // tm,),
        in_specs=[row, pl.BlockSpec((d, d), lambda i: (0, 0)), row, mod(d), vec, vec],
        out_specs=row,
        out_shape=jax.ShapeDtypeStruct((n, d), F32),
        compiler_params=_cparams(("arbitrary",)),
        name="proj_ln",
    )(o, w_bf, x, gate, g, b)


def _router_kernel(x_ref, sc_ref, sh_ref, w_ref, b_ref, idx_ref, p_ref, rank_ref, cnt_ref, carry_scr,
                   *, n_experts):
    i = pl.program_id(0)
    tm = x_ref.shape[0]

    @pl.when(i == 0)
    def _():
        carry_scr[...] = jnp.zeros_like(carry_scr)

    h = x_ref[...] * (1.0 + sc_ref[...]) + sh_ref[...]
    logits = _dot_3pass(h, w_ref[...]) + b_ref[...]
    lane = lax.broadcasted_iota(jnp.int32, (tm, LANES), 1)
    lane_f = lane.astype(F32)
    l = jnp.where(lane < n_experts, logits, -jnp.inf)
    idx_out = jnp.zeros((tm, LANES), jnp.int32)
    top_v = []
    top_i = []
    for k in range(TOP_K):
        mk = jnp.max(l, axis=1, keepdims=True)
        ik = jnp.min(jnp.where(l == mk, lane_f, float(LANES)), axis=1, keepdims=True).astype(jnp.int32)
        top_v.append(mk)
        top_i.append(ik)
        idx_out = jnp.where(lane == k, ik, idx_out)
        l = jnp.where(lane == ik, -jnp.inf, l)
    es = [jnp.exp(v - top_v[0]) for v in top_v]
    den = es[0] + es[1] + es[2] + es[3]
    p_out = jnp.zeros((tm, LANES), F32)
    for k in range(TOP_K):
        p_out = jnp.where(lane == k, es[k] / den, p_out)

    onehot = jnp.zeros((tm, LANES), F32)
    for k in range(TOP_K):
        onehot = onehot + (lane == top_i[k]).astype(F32)
    r = lax.broadcasted_iota(jnp.int32, (tm, tm), 0)
    c = lax.broadcasted_iota(jnp.int32, (tm, tm), 1)
    lower = (c < r).astype(BF16)
    before = _dot(lower, onehot.astype(BF16)) + carry_scr[0:1, :]
    rank_out = jnp.zeros((tm, LANES), jnp.int32)
    for k in range(TOP_K):
        rk = jnp.sum(jnp.where(lane == top_i[k], before, 0.0), axis=1, keepdims=True)
        rank_out = jnp.where(lane == k, rk.astype(jnp.int32), rank_out)
    total = carry_scr[0:1, :] + jnp.sum(onehot, axis=0, keepdims=True)
    carry_scr[...] = jnp.broadcast_to(total, carry_scr.shape)
    idx_ref[...] = idx_out
    p_ref[...] = p_out
    rank_ref[...] = rank_out
    cnt_ref[...] = jnp.broadcast_to(total, cnt_ref.shape)


def _router(x, sc, sh, w_pad, b_pad, *, n_experts, per_row, tm, tiles_per_group):
    n, d = x.shape
    mod = _mod_spec(per_row, tm, tiles_per_group)
    row = pl.BlockSpec((tm, d), lambda i: (i, 0))
    lrow = pl.BlockSpec((tm, LANES), lambda i: (i, 0))
    return pl.pallas_call(
        functools.partial(_router_kernel, n_experts=n_experts),
        grid=(n // tm,),
        in_specs=[row, mod(d), mod(d), pl.BlockSpec((d, LANES), lambda i: (0, 0)),
                  pl.BlockSpec((1, LANES), lambda i: (0, 0))],
        out_specs=[lrow, lrow, lrow, pl.BlockSpec((SUBLANES, LANES), lambda i: (0, 0))],
        out_shape=[jax.ShapeDtypeStruct((n, LANES), jnp.int32), jax.ShapeDtypeStruct((n, LANES), F32),
                   jax.ShapeDtypeStruct((n, LANES), jnp.int32),
                   jax.ShapeDtypeStruct((SUBLANES, LANES), F32)],
        scratch_shapes=[pltpu.VMEM((SUBLANES, LANES), F32)],
        compiler_params=_cparams(("arbitrary",)),
        name="router",
    )(x, sc, sh, w_pad, b_pad)


def _slab_rows(d):
    assert d % LANES == 0 and d // LANES == SUBLANES
    return d // LANES


def _store_slabs(ref, val, tm):
    for c in range(SUBLANES):
        ref[pl.ds(c, tm, stride=SUBLANES), :] = val[:, c * LANES:(c + 1) * LANES]


def _load_slabs(ref, tm, lead=()):
    return [ref[lead + (pl.ds(c, tm, stride=SUBLANES), slice(None))] for c in range(SUBLANES)]


def _dispatch_kernel(pad_start_ref, pad_len_ref, dest_ref, x_ref, sc_ref, sh_ref, xs_ref, hbuf, zbuf, sem, zsem,
                     *, n_experts):
    i = pl.program_id(0)
    tm = x_ref.shape[0]

    def zero_copy(e, j):
        row = pl.multiple_of((pad_start_ref[e] + j) * SUBLANES, SUBLANES)
        return pltpu.make_async_copy(zbuf, xs_ref.at[pl.ds(row, SUBLANES)], zsem)

    @pl.when(i == 0)
    def _():
        zbuf[...] = jnp.zeros_like(zbuf)
        for e in range(n_experts):
            lax.fori_loop(0, pad_len_ref[e], lambda j, _: (zero_copy(e, j).start(), 0)[1], 0)
        for e in range(n_experts):
            lax.fori_loop(0, pad_len_ref[e], lambda j, _: (zero_copy(e, j).wait(), 0)[1], 0)

    _store_slabs(hbuf, x_ref[...] * (1.0 + sc_ref[...]) + sh_ref[...], tm)

    def row_copy(r, k):
        src = pl.multiple_of(r * SUBLANES, SUBLANES)
        dst = pl.multiple_of(dest_ref[k * tm + r] * SUBLANES, SUBLANES)
        return pltpu.make_async_copy(hbuf.at[pl.ds(src, SUBLANES)], xs_ref.at[pl.ds(dst, SUBLANES)], sem)

    def start(r, _):
        for k in range(TOP_K):
            row_copy(r, k).start(priority=k % 2)
        return 0

    def wait(r, _):
        for k in range(TOP_K):
            row_copy(r, k).wait()
        return 0

    lax.fori_loop(0, tm, start, 0, unroll=8)
    lax.fori_loop(0, tm, wait, 0, unroll=8)


def _tile_major(dest, tm):
    n, k = dest.shape
    return dest.reshape(n // tm, tm, k).transpose(0, 2, 1).reshape(-1)


def _dispatch(pad_start, pad_len, dest, x, sc, sh, n_rows, *, per_row, tm, tiles_per_group):
    n, d = x.shape
    _slab_rows(d)
    mod = _mod_spec(per_row, tm, tiles_per_group)
    grid_spec = pltpu.PrefetchScalarGridSpec(
        num_scalar_prefetch=2,
        grid=(n // tm,),
        in_specs=[
            pl.BlockSpec((TOP_K * tm,), lambda i, *_: (i,), memory_space=pltpu.SMEM),
            pl.BlockSpec((tm, d), lambda i, *_: (i, 0)), mod(d), mod(d),
        ],
        out_specs=pl.BlockSpec(memory_space=pl.ANY),
        scratch_shapes=[pltpu.VMEM((tm * SUBLANES, LANES), F32), pltpu.VMEM((SUBLANES, LANES), F32),
                        pltpu.SemaphoreType.DMA, pltpu.SemaphoreType.DMA],
    )
    return pl.pallas_call(
        functools.partial(_dispatch_kernel, n_experts=pad_start.shape[0]),
        grid_spec=grid_spec,
        out_shape=jax.ShapeDtypeStruct((n_rows * SUBLANES, LANES), F32),
        compiler_params=_cparams(("arbitrary",)),
        name="dispatch",
    )(pad_start, pad_len, _tile_major(dest, tm), x, sc, sh)


def _experts_kernel(te_ref, nt_ref, xs_ref, wgu_ref, bgu_ref, wd_ref, bd_ref, ys_ref, wgu_bf, wd_bf, *, tmg):
    t = pl.program_id(0)
    de = wd_ref.shape[0]

    @pl.when((t == 0) | (te_ref[t] != te_ref[jnp.maximum(t - 1, 0)]))
    def _():
        wgu_bf[...] = wgu_ref[...].astype(BF16)
        wd_bf[...] = wd_ref[...].astype(BF16)

    @pl.when(t < nt_ref[0])
    def _():
        x = jnp.concatenate([c.astype(BF16) for c in _load_slabs(xs_ref, tmg)], axis=1)
        gu = _dot(x, wgu_bf[...]) + bgu_ref[...]
        gate = jnp.minimum(gu[:, :de], SWIGLU_LIMIT)
        up = jnp.clip(gu[:, de:], -SWIGLU_LIMIT, SWIGLU_LIMIT)
        act = (up + 1.0) * (gate * jax.nn.sigmoid(SWIGLU_ALPHA * gate))
        _store_slabs(ys_ref, _dot(act.astype(BF16), wd_bf[...]) + bd_ref[...], tmg)

    @pl.when(t >= nt_ref[0])
    def _():
        ys_ref[...] = jnp.zeros_like(ys_ref)


def _experts(tile_expert, n_tiles, xs, w_gate_up, b_gate_up, w_down, b_down, *, layer, tmg):
    _, n_exp, d, de2 = w_gate_up.shape
    de = w_down.shape[2]
    n_rows = xs.shape[0] // SUBLANES
    depth = w_gate_up.shape[0]
    last = lambda t, nt: jnp.minimum(t, nt[0] - 1)
    grid_spec = pltpu.PrefetchScalarGridSpec(
        num_scalar_prefetch=2,
        grid=(n_rows // tmg,),
        in_specs=[
            pl.BlockSpec((tmg * SUBLANES, LANES), lambda t, te, nt: (last(t, nt), 0)),
            pl.BlockSpec((None, None, d, de2), lambda t, te, nt: (layer, te[t], 0, 0)),
            pl.BlockSpec((None, None, 1, de2), lambda t, te, nt: (layer, te[t], 0, 0)),
            pl.BlockSpec((None, None, de, d), lambda t, te, nt: (layer, te[t], 0, 0)),
            pl.BlockSpec((None, None, 1, d), lambda t, te, nt: (layer, te[t], 0, 0)),
        ],
        out_specs=pl.BlockSpec((tmg * SUBLANES, LANES), lambda t, te, nt: (t, 0)),
        scratch_shapes=[pltpu.VMEM((d, de2), BF16), pltpu.VMEM((de, d), BF16)],
    )
    return pl.pallas_call(
        functools.partial(_experts_kernel, tmg=tmg),
        grid_spec=grid_spec,
        out_shape=jax.ShapeDtypeStruct((n_rows * SUBLANES, LANES), F32),
        compiler_params=_cparams(("arbitrary",)),
        name="experts",
    )(tile_expert, n_tiles, xs, w_gate_up, b_gate_up.reshape(depth, n_exp, 1, de2), w_down,
      b_down.reshape(depth, n_exp, 1, d))


def _combine_kernel(dest_ref, dest_next_ref, p_ref, x_ref, gate_ref, g_ref, b_ref, ys_ref, out_ref,
                    buf, sems, *, dn_alpha):
    i = pl.program_id(0)
    n = pl.num_programs(0)
    tm = x_ref.shape[0]
    slot = i % 2

    def row_copy(dref, s, r, k):
        src = pl.multiple_of(dref[k * tm + r] * SUBLANES, SUBLANES)
        dst = pl.multiple_of(r * SUBLANES, SUBLANES)
        return pltpu.make_async_copy(ys_ref.at[pl.ds(src, SUBLANES)], buf.at[s, k, pl.ds(dst, SUBLANES)],
                                     sems.at[s])

    def start_tile(dref, s):
        def body(r, _):
            for k in range(TOP_K):
                row_copy(dref, s, r, k).start(priority=k % 2)
            return 0
        lax.fori_loop(0, tm, body, 0, unroll=8)

    @pl.when(i == 0)
    def _():
        start_tile(dest_ref, 0)

    @pl.when(i + 1 < n)
    def _():
        start_tile(dest_next_ref, 1 - slot)

    def wait_body(r, _):
        for k in range(TOP_K):
            row_copy(dest_ref, slot, r, k).wait()
        return 0

    lax.fori_loop(0, tm, wait_body, 0, unroll=8)

    p = p_ref[...]
    pk = [p[:, k:k + 1] for k in range(TOP_K)]
    chunks = [_load_slabs(buf, tm, lead=(slot, k)) for k in range(TOP_K)]
    y = jnp.concatenate([sum(pk[k] * chunks[k][c] for k in range(TOP_K)) for c in range(SUBLANES)], axis=1)
    z = dn_alpha * x_ref[...] + gate_ref[...] * y
    out_ref[...] = _layer_norm(z, g_ref[...], b_ref[...])


def _combine(dest, p, x, gate, g, b, ys, *, dn_alpha, per_row, tm, tiles_per_group):
    n, d = x.shape
    nt = n // tm
    dest_t = _tile_major(dest, tm)
    mod = _mod_spec(per_row, tm, tiles_per_group)
    row = pl.BlockSpec((tm, d), lambda i: (i, 0))
    vec = pl.BlockSpec((1, d), lambda i: (0, 0))
    return pl.pallas_call(
        functools.partial(_combine_kernel, dn_alpha=dn_alpha),
        grid=(nt,),
        in_specs=[
            pl.BlockSpec((TOP_K * tm,), lambda i: (i,), memory_space=pltpu.SMEM),
            pl.BlockSpec((TOP_K * tm,), lambda i: (jnp.minimum(i + 1, nt - 1),), memory_space=pltpu.SMEM),
            pl.BlockSpec((tm, LANES), lambda i: (i, 0)),
            row, mod(d), vec, vec,
            pl.BlockSpec(memory_space=pl.ANY),
        ],
        out_specs=row,
        out_shape=jax.ShapeDtypeStruct((n, d), F32),
        scratch_shapes=[pltpu.VMEM((2, TOP_K, tm * SUBLANES, LANES), F32), pltpu.SemaphoreType.DMA((2,))],
        compiler_params=_cparams(("arbitrary",)),
        name="combine",
    )(dest_t, dest_t, p, x, gate, g, b, ys)


def _moe(x, sc, sh, gate, ln_g, ln_b, w_router, b_router, w_gate_up, b_gate_up, w_down, b_down,
         *, layer, dn_alpha, per_row, tm, tiles_per_group, tmg):
    n, d = x.shape
    n_exp = w_router.shape[1]
    w_pad = jnp.pad(w_router, ((0, 0), (0, LANES - n_exp)))
    b_pad = jnp.pad(b_router, (0, LANES - n_exp)).reshape(1, LANES)
    idx, p, rank, cnt = _router(x, sc, sh, w_pad, b_pad, n_experts=n_exp, per_row=per_row, tm=tm,
                                tiles_per_group=tiles_per_group)
    counts = cnt[0, :n_exp].astype(jnp.int32)
    padded = ((counts + tmg - 1) // tmg) * tmg
    ends = jnp.cumsum(padded)
    offsets = ends - padded
    n_rows = ((n * TOP_K + n_exp * (tmg - 1)) // tmg) * tmg
    top_i = idx[:, :TOP_K]
    experts = jnp.arange(n_exp, dtype=jnp.int32)
    base = jnp.sum(jnp.where(top_i[:, :, None] == experts, offsets, 0), axis=2)
    dest = base + rank[:, :TOP_K]
    n_tiles = (ends[-1] // tmg).astype(jnp.int32)
    tile_ids = jnp.minimum(jnp.arange(n_rows // tmg, dtype=jnp.int32), n_tiles - 1)
    tile_expert = jnp.sum((ends // tmg)[None, :] <= tile_ids[:, None], axis=1).astype(jnp.int32)
    tile_expert = jnp.minimum(tile_expert, n_exp - 1)

    xs = _dispatch(offsets + counts, padded - counts, dest, x, sc, sh, n_rows, per_row=per_row, tm=tm,
                   tiles_per_group=tiles_per_group)
    ys = _experts(tile_expert, n_tiles.reshape(1), xs, w_gate_up, b_gate_up, w_down, b_down, layer=layer, tmg=tmg)
    tmc = min(tm, 128)
    return _combine(dest, p, x, gate, ln_g, ln_b, ys, dn_alpha=dn_alpha, per_row=per_row, tm=tmc,
                    tiles_per_group=tiles_per_group * (tm // tmc))


def _expand_rows(q_flat, n_rows, group):
    d = q_flat.shape[1]
    r = lax.broadcasted_iota(jnp.int32, (n_rows, d), 0)
    c = lax.broadcasted_iota(jnp.int32, (n_rows, d), 1)
    q_rows = jnp.broadcast_to(q_flat.astype(F32), (n_rows, d))
    return jnp.where(c // group == r, q_rows, 0.0).astype(q_flat.dtype)


def _dec_a_kernel(pt_ref, q_ref, kn_ref, vn_ref, tabt_ref, lam_ref, g_ref, *rest,
                  lam_init, pages_per_step, page, past_len, n_heads):
    del pt_ref
    k_refs = rest[:pages_per_step]
    v_refs = rest[pages_per_step:2 * pages_per_step]
    o_ref = rest[2 * pages_per_step]
    m_scr, l_scr, acc_scr = rest[2 * pages_per_step + 1:]
    g = pl.program_id(1)
    ng = pl.num_programs(1)
    nr = 2 * n_heads
    d = q_ref.shape[1]
    dh = d // nr

    @pl.when(g == 0)
    def _():
        m_scr[...] = jnp.full_like(m_scr, NEG_BIG)
        l_scr[...] = jnp.zeros_like(l_scr)
        acc_scr[...] = jnp.zeros_like(acc_scr)

    qexp = _expand_rows(q_ref[...], nr, dh)
    assert page >= BUCKET_LO[N_BUCKETS - 1]
    tabt = tabt_ref[...] - tabt_ref[:, N_BUCKETS - 1:N_BUCKETS]
    t1, t2, t3 = _split3(tabt)
    bk = lax.broadcasted_iota(jnp.int32, (LANES, page), 0)
    kl = lax.broadcasted_iota(jnp.int32, (1, page), 1)
    sr = lax.broadcasted_iota(jnp.int32, (page, page * n_heads), 0)
    scol = lax.broadcasted_iota(jnp.int32, (page, page * n_heads), 1)
    spread = (scol // n_heads == sr).astype(BF16)
    kr = lax.broadcasted_iota(jnp.int32, (nr, page * n_heads), 0)
    kc = lax.broadcasted_iota(jnp.int32, (nr, page * n_heads), 1)
    keep = (kc % n_heads) == (kr % n_heads)

    mx = m_scr[...]
    l = l_scr[...]
    acc = acc_scr[...]
    s_pages = [_dot(qexp, k_refs[j][...].astype(BF16)) for j in range(pages_per_step)]
    pos = (g * pages_per_step + pages_per_step - 1) * page + kl
    bucket = _bias_from_dist(past_len - pos, lambda b: jnp.int32(b))
    onehot = (jnp.broadcast_to(bucket, (LANES, page)) == bk).astype(BF16)
    s_pages[-1] = s_pages[-1] + (_dot(t1, onehot) + _dot(t2, onehot) + _dot(t3, onehot))
    s_all = jnp.concatenate(s_pages, axis=1)
    mx_new = jnp.maximum(mx, jnp.max(s_all, axis=1, keepdims=True))
    alpha = jnp.exp(mx - mx_new)
    p_all = jnp.exp(s_all - mx_new)
    l = alpha * l + jnp.sum(p_all, axis=1, keepdims=True)
    acc = alpha * acc
    for j in range(pages_per_step):
        p = p_all[:, j * page:(j + 1) * page].astype(BF16)
        pe = jnp.where(keep, _dot(p, spread), 0.0).astype(BF16)
        acc = acc + _dot(pe, v_refs[j][...].astype(BF16))
    mx = mx_new
    m_scr[...] = mx
    l_scr[...] = l
    acc_scr[...] = acc

    @pl.when(g == ng - 1)
    def _():
        kexp = jnp.broadcast_to(kn_ref[...], (nr, d)).astype(F32)
        s_new = jnp.sum(qexp.astype(F32) * kexp, axis=1, keepdims=True) + tabt[:, 0:1]
        mx2 = jnp.maximum(mx, s_new)
        alpha = jnp.exp(mx - mx2)
        p_new = jnp.exp(s_new - mx2)
        l2 = alpha * l + p_new
        vn = vn_ref[...].astype(F32)
        acc2 = alpha * acc + p_new * jnp.concatenate([vn, vn], axis=0)
        o = acc2 / l2
        lam = _diff_lambda(lam_ref, lam_init)
        o = o[:n_heads] - lam * o[n_heads:]
        ms = jnp.mean(o * o, axis=1, keepdims=True)
        o = (o * lax.rsqrt(ms + LN_EPS)) * g_ref[...] * (1.0 - lam_init)
        o_ref[...] = o.astype(BF16)


def _dec_a(page_table, qb, kb_new, vb_new, rel_tab, lam_vecs, subln_g, kt_pages, v_pages,
           *, layer, lam_init, n_heads, page, pages_per_step):
    s_n, n_pages = page_table.shape
    d = qb.shape[1]
    nr = 2 * n_heads
    dv = d // n_heads
    past_len = n_pages * page
    tabt = jnp.pad(rel_tab.reshape(N_BUCKETS, nr).T, ((0, 0), (0, LANES - N_BUCKETS)))
    pps = pages_per_step
    page_spec = lambda j: pl.BlockSpec((None, None, kt_pages.shape[2], page),
                                       lambda s, g, pt: (layer, pt[s, g * pps + j], 0, 0))
    kernel = functools.partial(_dec_a_kernel, lam_init=lam_init, pages_per_step=pps, page=page,
                               past_len=past_len, n_heads=n_heads)
    grid_spec = pltpu.PrefetchScalarGridSpec(
        num_scalar_prefetch=1,
        grid=(s_n, n_pages // pps),
        in_specs=[
            pl.BlockSpec((None, 1, d), lambda s, g, pt: (s, 0, 0)),
            pl.BlockSpec((None, 1, d), lambda s, g, pt: (s, 0, 0)),
            pl.BlockSpec((None, n_heads, dv), lambda s, g, pt: (s, 0, 0)),
            pl.BlockSpec((nr, LANES), lambda s, g, pt: (0, 0)),
            pl.BlockSpec((4, lam_vecs.shape[1]), lambda s, g, pt: (0, 0)),
            pl.BlockSpec((1, dv), lambda s, g, pt: (0, 0)),
        ] + [page_spec(j) for j in range(pps)] + [page_spec(j) for j in range(pps)],
        out_specs=pl.BlockSpec((None, n_heads, dv), lambda s, g, pt: (s, 0, 0)),
        scratch_shapes=[pltpu.VMEM((nr, 1), F32), pltpu.VMEM((nr, 1), F32), pltpu.VMEM((nr, dv), F32)],
    )
    out = pl.pallas_call(
        kernel,
        grid_spec=grid_spec,
        out_shape=jax.ShapeDtypeStruct((s_n, n_heads, dv), BF16),
        compiler_params=_cparams(("arbitrary", "arbitrary")),
        name="dec_a",
    )(page_table, qb.reshape(s_n, 1, d), kb_new.reshape(s_n, 1, d), vb_new.reshape(s_n, n_heads, dv),
      tabt, lam_vecs, subln_g, *([kt_pages] * pps), *([v_pages] * pps))
    return out.reshape(s_n, d)


def _dec_b_kernel(pt_ref, q_ref, k_hbm, v_hbm, o_ref, kbuf, vbuf, sems, *, layer, n_pages, page, n_heads):
    s = pl.program_id(0)
    d = q_ref.shape[1]
    dh = d // n_heads

    def copies(pg, slot):
        phys = pt_ref[s, pg]
        return (pltpu.make_async_copy(k_hbm.at[layer, phys], kbuf.at[slot], sems.at[0, slot]),
                pltpu.make_async_copy(v_hbm.at[layer, phys], vbuf.at[slot], sems.at[1, slot]))

    def start(pg, slot):
        for c in copies(pg, slot):
            c.start()

    def wait(pg, slot):
        for c in copies(pg, slot):
            c.wait()

    qexp = _expand_rows(q_ref[...], n_heads, dh)
    row = lax.broadcasted_iota(jnp.int32, (page, page), 0)
    col = lax.broadcasted_iota(jnp.int32, (page, page), 1)
    upper = (row > col).astype(BF16)

    start(n_pages - 1, 0)

    def live(c):
        pg, tot, _ = c
        return (pg >= 0) & (jnp.max(tot) > SB_DEAD)

    def visit(c):
        pg, tot, acc = c
        slot = (n_pages - 1 - pg) % 2
        wait(pg, slot)

        @pl.when(pg > 0)
        def _():
            start(pg - 1, 1 - slot)

        z = _dot(qexp, kbuf[slot].astype(BF16))
        sp = _softplus(z)
        ls = -sp
        w = jnp.exp(z - sp + (tot + _suffix_sums(ls, upper)))
        acc = acc + _dot_nt(w.astype(BF16), vbuf[slot].astype(BF16))
        tot = tot + jnp.sum(ls, axis=1, keepdims=True)
        return pg - 1, tot, acc

    pg, _, acc = lax.while_loop(live, visit, (jnp.int32(n_pages - 1), jnp.zeros((n_heads, 1), F32),
                                             jnp.zeros((n_heads, d), F32)))

    @pl.when(pg >= 0)
    def _():
        wait(pg, (n_pages - 1 - pg) % 2)

    r = lax.broadcasted_iota(jnp.int32, (n_heads, d), 0)
    c = lax.broadcasted_iota(jnp.int32, (n_heads, d), 1)
    o_ref[...] = jnp.sum(jnp.where(c // dh == r, acc, 0.0), axis=0, keepdims=True).astype(BF16)


def _dec_b(page_table, qb, kt_pages, vt_pages, *, layer, n_heads, page):
    s_n, n_pages = page_table.shape
    d = qb.shape[1]
    grid_spec = pltpu.PrefetchScalarGridSpec(
        num_scalar_prefetch=1,
        grid=(s_n,),
        in_specs=[pl.BlockSpec((None, 1, d), lambda s, pt: (s, 0, 0)),
                  pl.BlockSpec(memory_space=pl.ANY), pl.BlockSpec(memory_space=pl.ANY)],
        out_specs=pl.BlockSpec((None, 1, d), lambda s, pt: (s, 0, 0)),
        scratch_shapes=[pltpu.VMEM((2, d, page), F32), pltpu.VMEM((2, d, page), F32),
                        pltpu.SemaphoreType.DMA((2, 2))],
    )
    out = pl.pallas_call(
        functools.partial(_dec_b_kernel, layer=layer, n_pages=n_pages, page=page, n_heads=n_heads),
        grid_spec=grid_spec,
        out_shape=jax.ShapeDtypeStruct((s_n, 1, d), BF16),
        compiler_params=_cparams(("arbitrary",)),
        name="dec_b",
    )(page_table, qb.reshape(s_n, 1, d), kt_pages, vt_pages)
    return out.reshape(s_n, d)


def kernel(x_prompt, x_sample, c_prompt, c_sample, cache_a_k, cache_a_v, cache_b_k, cache_b_v, page_table, rel_table, w_qkv_a, w_o_a, lambda_a, subln_a, w_qkv_b, w_o_b, w_ada, b_ada, ln_g, ln_b, w_router, b_router, w_gate_up, b_gate_up, w_down, b_down):
    batch, seq, d = x_prompt.shape
    s_n = x_sample.shape[0]
    depth = w_ada.shape[0]
    a_heads = cache_a_k.shape[4]
    a_dh = cache_a_k.shape[5]
    b_heads = cache_b_k.shape[3]
    b_dh = cache_b_k.shape[4]
    page = cache_a_k.shape[2]
    n_pool = cache_a_k.shape[1]
    dn_alpha = (2 * depth) ** 0.25
    n_p = batch * seq

    tm = min(512, seq)
    tiles_per_batch = seq // tm
    t_attn = min(256, seq)
    pps = min(16, page_table.shape[1])

    mods = _ada(jnp.concatenate([c_prompt, c_sample], axis=0), w_ada, b_ada)
    mods = mods.reshape(depth, batch + s_n, 6, d)

    xp = x_prompt.reshape(n_p, d)
    xs = x_sample.reshape(s_n, d)
    outs_p = {"a_k": [], "a_v": [], "b_k": [], "b_v": []}
    outs_s = {"a_k": [], "a_v": [], "b_k": [], "b_v": []}

    kt_a = jnp.transpose(cache_a_k, (0, 1, 3, 4, 5, 2)).reshape(-1, n_pool, 2 * a_heads * a_dh, page)
    v_a = cache_a_v.reshape(-1, n_pool, page * a_heads, 2 * a_dh)
    kt_b = jnp.transpose(cache_b_k, (0, 1, 3, 4, 2)).reshape(-1, n_pool, b_heads * b_dh, page)
    vt_b = jnp.transpose(cache_b_v, (0, 1, 3, 4, 2)).reshape(-1, n_pool, b_heads * b_dh, page)

    for i in range(depth):
        j = i // 2
        mp = [mods[i, :batch, c].reshape(batch, 1, d) for c in range(6)]
        ms = [mods[i, batch:, c] for c in range(6)]
        kw_p = dict(per_row=False, tm=tm, tiles_per_group=tiles_per_batch)
        kw_s = dict(per_row=True, tm=s_n, tiles_per_group=1)
        if i % 2 == 0:
            lam_init = 0.8 - 0.6 * math.exp(-0.3 * i)
            w_bf = w_qkv_a[j].astype(BF16)
            q_scale = a_dh ** -0.5
            qb, kb, vb, kf, vf = _qkv(xp, mp[1], mp[0], w_bf, q_scale=q_scale, k_t=True, **kw_p)
            o_p = _attn_a(qb, kb, vb, rel_table, lambda_a[j], subln_a[j].reshape(1, -1), batch=batch, seq=seq,
                          n_heads=a_heads, lam_init=lam_init, t=t_attn)
            qs, ks, vs, ksf, vsf = _qkv(xs, ms[1], ms[0], w_bf, q_scale=q_scale, **kw_s)
            o_s = _dec_a(page_table, qs, ks, vs, rel_table, lambda_a[j], subln_a[j].reshape(1, -1),
                         kt_a, v_a, layer=j, lam_init=lam_init, n_heads=a_heads, page=page, pages_per_step=pps)
            w_o = w_o_a[j].astype(BF16)
            outs_p["a_k"].append(jnp.transpose(kf.reshape(batch, 2, a_heads, a_dh, seq), (0, 4, 1, 2, 3)))
            outs_p["a_v"].append(vf.reshape(batch, seq, a_heads, 2 * a_dh))
            outs_s["a_k"].append(ksf.reshape(s_n, 1, 2, a_heads, a_dh))
            outs_s["a_v"].append(vsf.reshape(s_n, 1, a_heads, 2 * a_dh))
        else:
            w_bf = w_qkv_b[j].astype(BF16)
            q_scale = b_dh ** -0.5
            qb, kb, vb, kf, vf = _qkv(xp, mp[1], mp[0], w_bf, q_scale=q_scale, k_t=True, v_t=True, **kw_p)
            o_p = _attn_b(qb, kb, vb, batch=batch, seq=seq, n_heads=b_heads, t=t_attn)
            qs, ks, vs, ksf, vsf = _qkv(xs, ms[1], ms[0], w_bf, q_scale=q_scale, **kw_s)
            o_s = _dec_b(page_table, qs, kt_b, vt_b, layer=j, n_heads=b_heads, page=page)
            w_o = w_o_b[j].astype(BF16)
            outs_p["b_k"].append(jnp.transpose(kf.reshape(batch, b_heads, b_dh, seq), (0, 3, 1, 2)))
            outs_p["b_v"].append(jnp.transpose(vf.reshape(batch, b_heads, b_dh, seq), (0, 3, 1, 2)))
            outs_s["b_k"].append(ksf.reshape(s_n, 1, b_heads, b_dh))
            outs_s["b_v"].append(vsf.reshape(s_n, 1, b_heads, b_dh))
        g0 = ln_g[i, 0].reshape(1, d)
        b0 = ln_b[i, 0].reshape(1, d)
        g1 = ln_g[i, 1].reshape(1, d)
        b1 = ln_b[i, 1].reshape(1, d)
        xp = _proj_ln(o_p, w_o, xp, mp[2], g0, b0, dn_alpha=dn_alpha, **kw_p)
        xs = _proj_ln(o_s, w_o, xs, ms[2], g0, b0, dn_alpha=dn_alpha, **kw_s)
        moe_args = (w_router[i], b_router[i], w_gate_up, b_gate_up, w_down, b_down)
        xp = _moe(xp, mp[4], mp[3], mp[5], g1, b1, *moe_args, layer=i, dn_alpha=dn_alpha, tmg=512, **kw_p)
        xs = _moe(xs, ms[4], ms[3], ms[5], g1, b1, *moe_args, layer=i, dn_alpha=dn_alpha, tmg=8, **kw_s)

    return (xp.reshape(batch, seq, d), xs.reshape(s_n, 1, d),
            jnp.stack(outs_p["a_k"]), jnp.stack(outs_p["a_v"]), jnp.stack(outs_p["b_k"]), jnp.stack(outs_p["b_v"]),
            jnp.stack(outs_s["a_k"]), jnp.stack(outs_s["a_v"]), jnp.stack(outs_s["b_k"]), jnp.stack(outs_s["b_v"]))
```
